```python
import math
import jax
import jax.numpy as jnp
from jax import lax
import numpy as np

D_MODEL = 1024
BATCH = 8
SEQ = 2048
DEPTH = 4
DEC_BATCH = 32
DEC_SEQ = 8
PAST_LEN = 8192
PAGE_SIZE = 128

N_MIXERS = 3
A_GROUPS = ((128, 1), (512, 4), (2048, 16))
N_GROUPS = len(A_GROUPS)
H_G = 8
DH = 128
ATT_WIDTH = H_G * DH
BLK = 128
CONV_K = 31
C_WIDTH = D_MODEL
C_GROUPS = 8
C_GW = C_WIDTH // C_GROUPS
CHUNK = 128
D_FF = 2816
FFN_K = 3
EPS = 1e-6
NEG = -1e30
N_A = (DEPTH + 2) // 3
N_B = (DEPTH + 1) // 3
N_C = DEPTH // 3

kernel_name = 'hybrid_dilated_conformer_gmlp_convffn_step'


def _rms_norm(x, g):
    xf = x.astype(jnp.float32)
    y = xf * lax.rsqrt(jnp.mean(xf * xf, axis=-1, keepdims=True) + EPS)
    return (y * g.astype(jnp.float32)).astype(x.dtype)


def _layer_norm(x, g, b):
    xf = x.astype(jnp.float32)
    mu = jnp.mean(xf, axis=-1, keepdims=True)
    xc = xf - mu
    y = xc * lax.rsqrt(jnp.mean(xc * xc, axis=-1, keepdims=True) + EPS)
    return (y * g.astype(jnp.float32) + b.astype(jnp.float32)).astype(x.dtype)


def _causal_dwconv(xcat, w, b):
    y = lax.conv_general_dilated(xcat, w[:, None, :].astype(xcat.dtype), (1,), 'VALID',
                                 dimension_numbers=('NWC', 'WIO', 'NWC'),
                                 feature_group_count=xcat.shape[-1])
    return y + b


def _dilated_prompt(q, k, v, dil, band):
    nb_, s_ = q.shape[0], q.shape[1]
    n_sub = -(-s_ // dil)
    nblk = -(-n_sub // BLK)
    lp = nblk * BLK
    sp = lp * dil

    def to_res(t):
        t = jnp.pad(t.astype(jnp.float32), ((0, 0), (0, sp - s_), (0, 0), (0, 0)))
        return t.reshape(nb_, lp, dil, H_G, DH).transpose(0, 2, 1, 3, 4)

    def band_keys(t):
        t = jnp.pad(t, ((0, 0), (0, 0), (BLK, 0), (0, 0), (0, 0))).reshape(nb_, dil, nblk + 1, BLK, H_G, DH)
        return jnp.concatenate([t[:, :, :-1], t[:, :, 1:]], axis=3)

    qr = to_res(q).reshape(nb_, dil, nblk, BLK, H_G, DH)
    kr = band_keys(to_res(k))
    vr = band_keys(to_res(v))
    sc = jnp.einsum('bziqhe,bzikhe->bzihqk', qr, kr) * (DH ** -0.5)
    a_idx = jnp.arange(BLK)[:, None]
    c_idx = jnp.arange(2 * BLK)[None, :]
    dist = BLK + a_idx - c_idx
    valid = (dist >= 0) & (dist <= band)
    valid = valid[None] & ((jnp.arange(nblk)[:, None, None] > 0) | (c_idx[None] >= BLK))
    sc = jnp.where(valid[None, None, :, None], sc, NEG)
    lse = jax.nn.logsumexp(sc, axis=-1)
    p = jnp.exp(sc - lse[..., None])
    o = jnp.einsum('bzihqk,bzikhe->bziqhe', p, vr)
    o = o.reshape(nb_, dil, lp, H_G, DH).transpose(0, 2, 1, 3, 4).reshape(nb_, sp, H_G, DH)[:, :s_]
    lse = lse.transpose(0, 1, 2, 4, 3).reshape(nb_, dil, lp, H_G).transpose(0, 2, 1, 3).reshape(nb_, sp, H_G)[:, :s_]
    return o, lse


def _dilated_sample(q, kc, vc, dil, band, n_buf):
    t_ = q.shape[1]
    idx = n_buf + jnp.arange(t_)[:, None] - dil * jnp.arange(band + 1)[None, :]
    valid = idx >= 0
    idx = jnp.maximum(idx, 0)
    kg = kc[:, idx].astype(jnp.float32)
    vg = vc[:, idx].astype(jnp.float32)
    sc = jnp.einsum('bthe,btkhe->bhtk', q.astype(jnp.float32), kg) * (DH ** -0.5)
    sc = jnp.where(valid[None, None], sc, NEG)
    lse = jax.nn.logsumexp(sc, axis=-1)
    p = jnp.exp(sc - lse[..., None])
    o = jnp.einsum('bhtk,btkhe->bthe', p, vg)
    return o, lse.transpose(0, 2, 1)


def _merge_groups(outs, lses):
    o = jnp.stack(outs, axis=0)
    w = jax.nn.softmax(jnp.stack(lses, axis=0), axis=0)
    return jnp.sum(w[..., None] * o, axis=0)


def _mixer_a(xn, w_in, q_gain, k_gain, w_out, bufs):
    nb_, l_ = xn.shape[0], xn.shape[1]
    qkv = (xn @ w_in).reshape(nb_, l_, N_GROUPS, 3, H_G, DH)
    q = _rms_norm(qkv[:, :, :, 0], q_gain[:, None, :])
    k = _rms_norm(qkv[:, :, :, 1], k_gain[:, None, :])
    v = qkv[:, :, :, 2]
    outs, lses, new = [], [], []
    for g, (win, dil) in enumerate(A_GROUPS):
        band = win // dil
        kv_new = jnp.stack([k[:, :, g], v[:, :, g]], axis=2)
        if bufs is None:
            o, l = _dilated_prompt(q[:, :, g], k[:, :, g], v[:, :, g], dil, band)
            kv = kv_new
            keep = min(win, l_)
        else:
            n_buf = bufs[g].shape[1]
            kv = jnp.concatenate([bufs[g], kv_new], axis=1)
            o, l = _dilated_sample(q[:, :, g], kv[:, :, 0], kv[:, :, 1], dil, band, n_buf)
            keep = min(win, n_buf + l_)
        outs.append(o)
        lses.append(l)
        new.append(kv[:, kv.shape[1] - keep:])
    y = _merge_groups(outs, lses).astype(xn.dtype).reshape(nb_, l_, ATT_WIDTH) @ w_out
    return y, new


def _mixer_b(xn, w1, b1, wdw, bdw, lng, lnb, w2, b2, buf):
    h = xn @ w1 + b1
    u = h[..., :D_MODEL] * jax.nn.sigmoid(h[..., D_MODEL:])
    if buf is None:
        buf = jnp.zeros((u.shape[0], CONV_K - 1, D_MODEL), u.dtype)
    cat = jnp.concatenate([buf, u], axis=1)
    c = _causal_dwconv(cat, wdw, bdw)
    c = jax.nn.silu(_layer_norm(c, lng, lnb))
    return c @ w2 + b2, cat[:, cat.shape[1] - (CONV_K - 1):]


def _mixer_c(xn, w_in, lng, lnb, w_s, b_s, w_out):
    nb_, l_ = xn.shape[0], xn.shape[1]
    h = jax.nn.gelu(xn @ w_in, approximate=False)
    u = h[..., :C_WIDTH]
    v = _layer_norm(h[..., C_WIDTH:], lng, lnb)
    ws = w_s * jnp.tril(jnp.ones((CHUNK, CHUNK), w_s.dtype))
    n_chunk = -(-l_ // CHUNK)
    vp = jnp.pad(v, ((0, 0), (0, n_chunk * CHUNK - l_), (0, 0))).reshape(nb_, n_chunk, CHUNK, C_GROUPS, C_GW)
    s = jnp.einsum('gts,bcsgd->bctgd', ws, vp) + b_s.T[:, :, None]
    s = s.reshape(nb_, n_chunk * CHUNK, C_WIDTH)[:, :l_]
    return (u * s) @ w_out, v


def _conv_ffn(xn, w_up, w_dw, b_dw, w_down, buf):
    h = xn @ w_up
    g, u = h[..., :D_FF], h[..., D_FF:]
    if buf is None:
        buf = jnp.zeros((g.shape[0], FFN_K - 1, D_FF), g.dtype)
    cat = jnp.concatenate([buf, g], axis=1)
    gc = _causal_dwconv(cat, w_dw, b_dw)
    return (jax.nn.silu(gc) * u) @ w_down, cat[:, cat.shape[1] - (FFN_K - 1):]


def _trunk(x, a_cache, b_cache, f_cache, W):
    ia = ib = ic = 0
    new_a = [[] for _ in range(N_GROUPS)]
    new_b, new_c, new_f = [], [], []
    for layer in range(DEPTH):
        kind = layer % N_MIXERS
        if kind == 0:
            xn = _rms_norm(x, W['a_norm'][ia])
            bufs = None if a_cache is None else [c[ia] for c in a_cache]
            y, new = _mixer_a(xn, W['a_w_in'][ia], W['a_q_gain'][ia], W['a_k_gain'][ia], W['a_w_out'][ia], bufs)
            for g in range(N_GROUPS):
                new_a[g].append(new[g])
            ia += 1
        elif kind == 1:
            xn = _rms_norm(x, W['b_norm'][ib])
            buf = None if b_cache is None else b_cache[ib]
            y, nb = _mixer_b(xn, W['b_w_pw1'][ib], W['b_b_pw1'][ib], W['b_w_dw'][ib], W['b_b_dw'][ib],
                             W['b_ln_g'][ib], W['b_ln_b'][ib], W['b_w_pw2'][ib], W['b_b_pw2'][ib], buf)
            new_b.append(nb)
            ib += 1
        else:
            xn = _rms_norm(x, W['c_norm'][ic])
            y, nv = _mixer_c(xn, W['c_w_in'][ic], W['c_ln_g'][ic], W['c_ln_b'][ic], W['c_w_s'][ic],
                             W['c_b_s'][ic], W['c_w_out'][ic])
            new_c.append(nv)
            ic += 1
        x = x + y
        xn = _rms_norm(x, W['f_norm'][layer])
        fbuf = None if f_cache is None else f_cache[layer]
        y, nf = _conv_ffn(xn, W['f_w_up'][layer], W['f_w_dw'][layer], W['f_b_dw'][layer], W['f_w_down'][layer], fbuf)
        new_f.append(nf)
        x = x + y
    return (x, [jnp.stack(a, axis=0) for a in new_a], jnp.stack(new_b, axis=0),
            jnp.stack(new_c, axis=0), jnp.stack(new_f, axis=0))


def setup_inputs(seed: int = 0) -> dict:
    key = jax.random.key(seed)
    ks = iter(jax.random.split(key, 40))

    def nrm(shape, scale):
        return jax.random.normal(next(ks), shape, jnp.float32) * scale

    def gain(shape):
        return 1.0 + nrm(shape, 0.02)

    d = D_MODEL
    inp = {}
    inp['x_prompt'] = nrm((BATCH, SEQ, d), 1.0)
    inp['x_sample'] = nrm((DEC_BATCH, DEC_SEQ, d), 1.0)
    for g, (win, dil) in enumerate(A_GROUPS):
        inp['cache_a_g%d_kv' % g] = nrm((N_A, DEC_BATCH, min(win, PAST_LEN), 2, H_G, DH), 1.0)
    inp['state_b_conv'] = nrm((N_B, DEC_BATCH, CONV_K - 1, d), 0.5)
    inp['state_ffn_conv'] = nrm((DEPTH, DEC_BATCH, FFN_K - 1, D_FF), 1.0)
    inp['a_norm'] = gain((N_A, d))
    inp['a_w_in'] = nrm((N_A, d, N_GROUPS * 3 * H_G * DH), d ** -0.5)
    inp['a_q_gain'] = gain((N_A, N_GROUPS, DH))
    inp['a_k_gain'] = gain((N_A, N_GROUPS, DH))
    inp['a_w_out'] = nrm((N_A, ATT_WIDTH, d), ATT_WIDTH ** -0.5)
    inp['b_norm'] = gain((N_B, d))
    inp['b_w_pw1'] = nrm((N_B, d, 2 * d), d ** -0.5)
    inp['b_b_pw1'] = nrm((N_B, 2 * d), 0.02)
    inp['b_w_dw'] = nrm((N_B, CONV_K, d), CONV_K ** -0.5)
    inp['b_b_dw'] = nrm((N_B, d), 0.02)
    inp['b_ln_g'] = gain((N_B, d))
    inp['b_ln_b'] = nrm((N_B, d), 0.02)
    inp['b_w_pw2'] = nrm((N_B, d, d), d ** -0.5)
    inp['b_b_pw2'] = nrm((N_B, d), 0.02)
    inp['c_norm'] = gain((N_C, d))
    inp['c_w_in'] = nrm((N_C, d, 2 * C_WIDTH), d ** -0.5)
    inp['c_ln_g'] = gain((N_C, C_WIDTH))
    inp['c_ln_b'] = nrm((N_C, C_WIDTH), 0.02)
    inp['c_w_s'] = nrm((N_C, C_GROUPS, CHUNK, CHUNK), CHUNK ** -0.5)
    inp['c_b_s'] = gain((N_C, C_GROUPS, CHUNK))
    inp['c_w_out'] = nrm((N_C, C_WIDTH, d), C_WIDTH ** -0.5)
    inp['f_norm'] = gain((DEPTH, d))
    inp['f_w_up'] = nrm((DEPTH, d, 2 * D_FF), d ** -0.5)
    inp['f_w_dw'] = nrm((DEPTH, FFN_K, D_FF), FFN_K ** -0.5)
    inp['f_b_dw'] = nrm((DEPTH, D_FF), 0.02)
    inp['f_w_down'] = nrm((DEPTH, D_FF, d), D_FF ** -0.5)
    return inp


def reference(x_prompt, x_sample, cache_a_g0_kv, cache_a_g1_kv, cache_a_g2_kv, state_b_conv, state_ffn_conv,
              a_norm, a_w_in, a_q_gain, a_k_gain, a_w_out,
              b_norm, b_w_pw1, b_b_pw1, b_w_dw, b_b_dw, b_ln_g, b_ln_b, b_w_pw2, b_b_pw2,
              c_norm, c_w_in, c_ln_g, c_ln_b, c_w_s, c_b_s, c_w_out,
              f_norm, f_w_up, f_w_dw, f_b_dw, f_w_down):
    W = dict(a_norm=a_norm, a_w_in=a_w_in, a_q_gain=a_q_gain, a_k_gain=a_k_gain, a_w_out=a_w_out,
             b_norm=b_norm, b_w_pw1=b_w_pw1, b_b_pw1=b_b_pw1, b_w_dw=b_w_dw, b_b_dw=b_b_dw,
             b_ln_g=b_ln_g, b_ln_b=b_ln_b, b_w_pw2=b_w_pw2, b_b_pw2=b_b_pw2,
             c_norm=c_norm, c_w_in=c_w_in, c_ln_g=c_ln_g, c_ln_b=c_ln_b, c_w_s=c_w_s, c_b_s=c_b_s,
             c_w_out=c_w_out, f_norm=f_norm, f_w_up=f_w_up, f_w_dw=f_w_dw, f_b_dw=f_b_dw, f_w_down=f_w_down)
    y_prompt, pa, pb, _pc, pf = _trunk(x_prompt, None, None, None, W)
    y_sample, sa, sb, sc, sf = _trunk(x_sample, [cache_a_g0_kv, cache_a_g1_kv, cache_a_g2_kv],
                                      state_b_conv, state_ffn_conv, W)
    return (y_prompt, y_sample, pa[0], pa[1], pa[2], pb, pf, sa[0], sa[1], sa[2], sb, sc, sf)
```

```python
import functools

import jax
import jax.numpy as jnp
from jax import lax
from jax.experimental import pallas as pl
from jax.experimental.pallas import tpu as pltpu

F32 = jnp.float32
BF16 = jnp.bfloat16

D_MODEL = 1024
DEPTH = 4
A_GROUPS = ((128, 1), (512, 4), (2048, 16))
N_GROUPS = 3
H_G = 8
DH = 128
ATT_BLK = 128
CONV_K = 31
C_GROUPS = 8
CHUNK = 128
D_FF = 2816
FFN_K = 3
EPS = 1e-6
NEG = -1e30

LANES = 128
F32_SUBLANES = 8
BF16_SUBLANES = 16
VMEM_LIMIT_BYTES = 56 * 1024 * 1024

B_HALO = 32
F_HALO = 16


def _params(n_axes):
    return pltpu.CompilerParams(dimension_semantics=("arbitrary",) * n_axes,
                                vmem_limit_bytes=VMEM_LIMIT_BYTES)


def _rms_norm(x, g):
    return x * lax.rsqrt(jnp.mean(x * x, axis=-1, keepdims=True) + EPS) * g


def _layer_norm(x, g, b):
    mu = jnp.mean(x, axis=-1, keepdims=True)
    xc = x - mu
    return xc * lax.rsqrt(jnp.mean(xc * xc, axis=-1, keepdims=True) + EPS) * g + b


def _idiv(x, n):
    assert n & (n - 1) == 0
    return jnp.right_shift(x, n.bit_length() - 1)


def _imod(x, n):
    assert n & (n - 1) == 0
    return jnp.bitwise_and(x, n - 1)


def _dot(a, b):
    return jnp.dot(a, b, preferred_element_type=F32)


def _dot_nt(a, b):
    return lax.dot_general(a, b, (((1,), (1,)), ((), ())), preferred_element_type=F32)


def _qkv_kernel(x_ref, g_ref, w_ref, gain_ref, o_ref, xn_ref):
    j = pl.program_id(1)

    @pl.when(j == 0)
    def _():
        xn_ref[...] = _rms_norm(x_ref[...], g_ref[...]).astype(BF16)

    y = _dot(xn_ref[...], w_ref[...])
    sec = j % 3

    @pl.when(sec < 2)
    def _():
        gain = gain_ref[0]
        for h in range(H_G):
            yh = y[:, h * DH:(h + 1) * DH]
            ms = jnp.mean(yh * yh, axis=-1, keepdims=True)
            o_ref[:, h * DH:(h + 1) * DH] = yh * lax.rsqrt(ms + EPS) * gain

    @pl.when(sec == 2)
    def _():
        o_ref[...] = y


def _qkv_proj(x2d, norm_g, w_bf, gains, tm):
    m = x2d.shape[0]
    n_blk = N_GROUPS * 3
    wid = H_G * DH
    return pl.pallas_call(
        _qkv_kernel,
        grid=(m // tm, n_blk),
        in_specs=[pl.BlockSpec((tm, D_MODEL), lambda i, j: (i, 0)),
                  pl.BlockSpec((1, D_MODEL), lambda i, j: (0, 0)),
                  pl.BlockSpec((D_MODEL, wid), lambda i, j: (0, j)),
                  pl.BlockSpec((1, 1, DH), lambda i, j: (j, 0, 0))],
        out_specs=pl.BlockSpec((tm, wid), lambda i, j: (i, j)),
        out_shape=jax.ShapeDtypeStruct((m, n_blk * wid), F32),
        scratch_shapes=[pltpu.VMEM((tm, D_MODEL), BF16)],
        compiler_params=_params(2),
        name="qkv_proj",
    )(x2d, norm_g, w_bf, gains)


def _attn_prompt_kernel(q0, k0, v0, q1, k1, v1, q2, k2, v2, o_ref, acc_ref, m_ref, l_ref, *, seq):
    scale = DH ** -0.5
    blk = ATT_BLK
    a_i = lax.broadcasted_iota(jnp.int32, (blk, 2 * blk), 0)
    c_i = lax.broadcasted_iota(jnp.int32, (blk, 2 * blk), 1)
    mask_two = ((c_i >= blk) & ((c_i - blk) <= a_i)) | ((c_i < blk) & (c_i >= a_i))
    a_1 = lax.broadcasted_iota(jnp.int32, (blk, blk), 0)
    c_1 = lax.broadcasted_iota(jnp.int32, (blk, blk), 1)
    mask_one = c_1 <= a_1

    groups = ((q0, k0, v0), (q1, k1, v1), (q2, k2, v2))
    for g, (q_ref, k_ref, v_ref) in enumerate(groups):
        dil = A_GROUPS[g][1]
        n_sub = seq // dil
        n_blk = n_sub // blk
        for r in range(dil):
            for i in range(n_blk):
                def rows(first_blk, n):
                    start = r + first_blk * blk * dil
                    if dil == 1:
                        return pl.ds(start, n)
                    return pl.ds(start, n, stride=dil)

                rq = rows(i, blk)
                q = q_ref[0, rq, :].astype(BF16)
                if i == 0:
                    rk, mask = rq, mask_one
                else:
                    rk, mask = rows(i - 1, 2 * blk), mask_two
                k = k_ref[0, rk, :].astype(BF16)
                v = v_ref[0, rk, :].astype(BF16)
                s = _dot_nt(q, k) * scale
                s = jnp.where(mask, s, NEG)
                m_blk = jnp.max(s, axis=-1, keepdims=True)
                if g == 0:
                    p = jnp.where(mask, jnp.exp(s - m_blk), 0.0)
                    acc_ref[rq, :] = _dot(p.astype(BF16), v)
                    m_ref[rq, :] = jnp.broadcast_to(m_blk, (blk, LANES))
                    l_ref[rq, :] = jnp.broadcast_to(jnp.sum(p, axis=-1, keepdims=True), (blk, LANES))
                else:
                    m_prev = m_ref[rq, :]
                    m_new = jnp.maximum(m_prev, m_blk)
                    alpha = jnp.exp(m_prev - m_new)
                    p = jnp.where(mask, jnp.exp(s - m_new[:, 0:1]), 0.0)
                    acc_ref[rq, :] = alpha * acc_ref[rq, :] + _dot(p.astype(BF16), v)
                    l_ref[rq, :] = alpha * l_ref[rq, :] + jnp.sum(p, axis=-1, keepdims=True)
                    m_ref[rq, :] = m_new
    o_ref[0] = (acc_ref[...] / l_ref[...]).astype(o_ref.dtype)


def _attn_prompt(qkv3):
    nb, seq, _ = qkv3.shape
    in_specs = []
    for g in range(N_GROUPS):
        for sec in range(3):
            col = (g * 3 + sec) * H_G
            in_specs.append(pl.BlockSpec((1, seq, DH), lambda b, h, c=col: (b, 0, c + h)))
    return pl.pallas_call(
        functools.partial(_attn_prompt_kernel, seq=seq),
        grid=(nb, H_G),
        in_specs=in_specs,
        out_specs=pl.BlockSpec((1, seq, DH), lambda b, h: (b, 0, h)),
        out_shape=jax.ShapeDtypeStruct((nb, seq, H_G * DH), BF16),
        scratch_shapes=[pltpu.VMEM((seq, DH), F32), pltpu.VMEM((seq, LANES), F32),
                        pltpu.VMEM((seq, LANES), F32)],
        compiler_params=_params(2),
        name="attn_prompt",
    )(*([qkv3] * 9))


def _attn_sample_kernel(*refs, dil, band, n_buf, tc, n_c, t_new):
    if n_c > 1:
        q_ref, kn_ref, vn_ref, c_ref, nxt_ref, o_ref, lse_ref, cout_ref, m_ref, l_ref, acc_ref = refs
    else:
        q_ref, kn_ref, vn_ref, c_ref, o_ref, lse_ref, cout_ref, m_ref, l_ref, acc_ref = refs
        nxt_ref = None
    c = pl.program_id(1)
    scale = DH ** -0.5
    wid = H_G * DH
    n_q = H_G * t_new

    q = q_ref[0]
    qt = jnp.concatenate([q] * H_G, axis=0)
    row_h = _idiv(lax.broadcasted_iota(jnp.int32, (n_q, wid), 0), t_new)
    col_h = _idiv(lax.broadcasted_iota(jnp.int32, (n_q, wid), 1), DH)
    qbd = jnp.where(row_h == col_h, qt, 0.0).astype(BF16)

    @pl.when(c == 0)
    def _():
        m_ref[...] = jnp.full(m_ref.shape, NEG, F32)
        l_ref[...] = jnp.zeros(l_ref.shape, F32)
        acc_ref[...] = jnp.zeros(acc_ref.shape, F32)

    def update(keys, vals, idx):
        t = _imod(lax.broadcasted_iota(jnp.int32, idx.shape, 0), t_new)
        back = n_buf + t - idx
        valid = (back >= 0) & (back <= dil * band) & (_imod(back, dil) == 0)
        s = jnp.where(valid, _dot_nt(qbd, keys) * scale, NEG)
        m_prev = m_ref[...]
        m_new = jnp.maximum(m_prev, jnp.max(s, axis=-1, keepdims=True))
        alpha = jnp.exp(m_prev - m_new)
        p = jnp.where(valid, jnp.exp(s - m_new), 0.0)
        l_ref[...] = alpha * l_ref[...] + jnp.sum(p, axis=-1, keepdims=True)
        acc_ref[...] = alpha * acc_ref[...] + _dot(p.astype(BF16), vals)
        m_ref[...] = m_new

    idx_c = c * tc + lax.broadcasted_iota(jnp.int32, (n_q, tc), 1)
    update(c_ref[0, :, 0:wid].astype(BF16), c_ref[0, :, wid:2 * wid].astype(BF16), idx_c)

    cout_ref[0, 0:tc - t_new, :] = c_ref[0, t_new:tc, :]
    last = c == n_c - 1

    if n_c > 1:
        @pl.when(jnp.logical_not(last))
        def _():
            cout_ref[0, tc - t_new:tc, :] = nxt_ref[0]

    @pl.when(last)
    def _():
        kn = kn_ref[0]
        vn = vn_ref[0]
        cout_ref[0, tc - t_new:tc, 0:wid] = kn
        cout_ref[0, tc - t_new:tc, wid:2 * wid] = vn
        pad = jnp.zeros((LANES - t_new, wid), F32)
        idx_n = n_buf + lax.broadcasted_iota(jnp.int32, (n_q, LANES), 1)
        update(jnp.concatenate([kn, pad], axis=0).astype(BF16),
               jnp.concatenate([vn, pad], axis=0).astype(BF16), idx_n)
        l = l_ref[...]
        o_full = acc_ref[...] / l
        lse = m_ref[...] + jnp.log(l)
        for h in range(H_G):
            o_ref[0, :, h * DH:(h + 1) * DH] = o_full[h * t_new:(h + 1) * t_new, h * DH:(h + 1) * DH]
            lse_ref[0, :, h * DH:(h + 1) * DH] = jnp.broadcast_to(lse[h * t_new:(h + 1) * t_new], (t_new, DH))


def _attn_sample(qkv3, cache4, layer, g):
    ndb, t_new, _ = qkv3.shape
    n_buf = cache4.shape[2]
    win, dil = A_GROUPS[g]
    band = win // dil
    wid = H_G * DH
    tc = min(n_buf, 512)
    n_c = n_buf // tc
    in_specs = [pl.BlockSpec((1, t_new, wid), lambda b, c: (b, 0, g * 3)),
                pl.BlockSpec((1, t_new, wid), lambda b, c: (b, 0, g * 3 + 1)),
                pl.BlockSpec((1, t_new, wid), lambda b, c: (b, 0, g * 3 + 2)),
                pl.BlockSpec((None, 1, tc, 2 * wid), lambda b, c: (layer, b, c, 0))]
    args = [qkv3, qkv3, qkv3, cache4]
    if n_c > 1:
        per = tc // t_new
        last_blk = n_buf // t_new - 1
        in_specs.append(pl.BlockSpec((None, 1, t_new, 2 * wid),
                                     lambda b, c: (layer, b, jnp.minimum((c + 1) * per, last_blk), 0)))
        args.append(cache4)
    n_q = H_G * t_new
    return pl.pallas_call(
        functools.partial(_attn_sample_kernel, dil=dil, band=band, n_buf=n_buf, tc=tc, n_c=n_c, t_new=t_new),
        grid=(ndb, n_c),
        in_specs=in_specs,
        out_specs=[pl.BlockSpec((1, t_new, wid), lambda b, c: (b, 0, 0)),
                   pl.BlockSpec((1, t_new, wid), lambda b, c: (b, 0, 0)),
                   pl.BlockSpec((1, tc, 2 * wid), lambda b, c: (b, c, 0))],
        out_shape=[jax.ShapeDtypeStruct((ndb, t_new, wid), F32),
                   jax.ShapeDtypeStruct((ndb, t_new, wid), F32),
                   jax.ShapeDtypeStruct((ndb, n_buf, 2 * wid), F32)],
        scratch_shapes=[pltpu.VMEM((n_q, 1), F32), pltpu.VMEM((n_q, 1), F32), pltpu.VMEM((n_q, wid), F32)],
        compiler_params=_params(2),
        name="attn_sample_g%d" % g,
    )(*args)


def _outproj_kernel(h_ref, w_ref, res_ref, o_ref):
    o_ref[...] = res_ref[...] + _dot(h_ref[...], w_ref[...])


def _outproj(h2d, w_bf, res2d, tm):
    m = h2d.shape[0]
    return pl.pallas_call(
        _outproj_kernel,
        grid=(m // tm,),
        in_specs=[pl.BlockSpec((tm, D_MODEL), lambda i: (i, 0)),
                  pl.BlockSpec((D_MODEL, D_MODEL), lambda i: (0, 0)),
                  pl.BlockSpec((tm, D_MODEL), lambda i: (i, 0))],
        out_specs=pl.BlockSpec((tm, D_MODEL), lambda i: (i, 0)),
        out_shape=jax.ShapeDtypeStruct((m, D_MODEL), F32),
        compiler_params=_params(1),
        name="a_outproj",
    )(h2d, w_bf, res2d)


def _merge_outproj_kernel(o0, l0, o1, l1, o2, l2, w_ref, res_ref, o_ref):
    la, lb, lc = l0[...], l1[...], l2[...]
    mx = jnp.maximum(jnp.maximum(la, lb), lc)
    ea, eb, ec = jnp.exp(la - mx), jnp.exp(lb - mx), jnp.exp(lc - mx)
    att = (ea * o0[...] + eb * o1[...] + ec * o2[...]) / (ea + eb + ec)
    o_ref[...] = res_ref[...] + _dot(att.astype(BF16), w_ref[...])


def _merge_outproj(parts, w_bf, res2d):
    m = res2d.shape[0]
    row = pl.BlockSpec((m, D_MODEL), lambda i: (0, 0))
    return pl.pallas_call(
        _merge_outproj_kernel,
        grid=(1,),
        in_specs=[row] * 6 + [pl.BlockSpec((D_MODEL, D_MODEL), lambda i: (0, 0)), row],
        out_specs=row,
        out_shape=jax.ShapeDtypeStruct((m, D_MODEL), F32),
        compiler_params=_params(1),
        name="a_merge_outproj",
    )(*parts, w_bf, res2d)


def _glu_kernel(x_ref, g_ref, wa_ref, wb_ref, ba_ref, bb_ref, o_ref, xn_ref):
    @pl.when(pl.program_id(1) == 0)
    def _():
        xn_ref[...] = _rms_norm(x_ref[...], g_ref[...]).astype(BF16)

    xn = xn_ref[...]
    a = _dot(xn, wa_ref[...]) + ba_ref[...]
    b = _dot(xn, wb_ref[...]) + bb_ref[...]
    o_ref[...] = a * jax.nn.sigmoid(b)


def _glu_proj(x2d, norm_g, w_bf, bias, tm, tn):
    m = x2d.shape[0]
    nb = D_MODEL // tn
    return pl.pallas_call(
        _glu_kernel,
        grid=(m // tm, nb),
        in_specs=[pl.BlockSpec((tm, D_MODEL), lambda i, j: (i, 0)),
                  pl.BlockSpec((1, D_MODEL), lambda i, j: (0, 0)),
                  pl.BlockSpec((D_MODEL, tn), lambda i, j: (0, j)),
                  pl.BlockSpec((D_MODEL, tn), lambda i, j: (0, j + nb)),
                  pl.BlockSpec((1, tn), lambda i, j: (0, j)),
                  pl.BlockSpec((1, tn), lambda i, j: (0, j + nb))],
        out_specs=pl.BlockSpec((tm, tn), lambda i, j: (i, j)),
        out_shape=jax.ShapeDtypeStruct((m, D_MODEL), F32),
        scratch_shapes=[pltpu.VMEM((tm, D_MODEL), BF16)],
        compiler_params=_params(2),
        name="b_glu_proj",
    )(x2d, norm_g, w_bf, w_bf, bias, bias)


def _bconv_kernel(*refs, bs, tl, n_lt):
    if n_lt > 1:
        (u_ref, halo_ref, hist_ref, wdw_ref, bdw_ref, lng_ref, lnb_ref, w2_ref, b2_ref, res_ref,
         o_ref, cat_ref, y_ref) = refs
    else:
        (u_ref, hist_ref, wdw_ref, bdw_ref, lng_ref, lnb_ref, w2_ref, b2_ref, res_ref,
         o_ref, cat_ref, y_ref) = refs
    if n_lt > 1:
        first = pl.program_id(1) == 0

        @pl.when(first)
        def _():
            cat_ref[:, 0:B_HALO, :] = hist_ref[...]

        @pl.when(jnp.logical_not(first))
        def _():
            cat_ref[:, 0:B_HALO, :] = halo_ref[...]
    else:
        cat_ref[:, 0:B_HALO, :] = hist_ref[...]
    cat_ref[:, B_HALO:B_HALO + tl, :] = u_ref[...]

    rc = min(tl, 32)
    lc = 256
    n_rc = tl // rc
    lead = B_HALO - (CONV_K - 1)

    def body(it, carry):
        b = it // n_rc
        r0 = pl.multiple_of((it % n_rc) * rc, F32_SUBLANES)
        pieces = []
        for c0 in range(0, D_MODEL, lc):
            win = cat_ref[b, pl.ds(r0, rc + B_HALO), c0:c0 + lc]
            acc = jnp.broadcast_to(bdw_ref[:, c0:c0 + lc], (rc, lc))
            for k in range(CONV_K):
                acc = acc + win[lead + k:lead + k + rc, :] * wdw_ref[k:k + 1, c0:c0 + lc]
            pieces.append(acc)
        conv = jnp.concatenate(pieces, axis=1)
        y = _layer_norm(conv, lng_ref[...], lnb_ref[...])
        y_ref[pl.ds(pl.multiple_of(b * tl + r0, F32_SUBLANES), rc), :] = y * jax.nn.sigmoid(y)
        return carry

    lax.fori_loop(0, bs * n_rc, body, 0)
    o_ref[...] = res_ref[...] + _dot(y_ref[...].astype(BF16), w2_ref[...]) + b2_ref[...]


def _bconv(u3, hist, wdw, bdw, lng, lnb, w2_bf, b2, res2d, bs, tl):
    n, seq, _ = u3.shape
    n_lt = seq // tl
    rows = bs * tl
    vec = pl.BlockSpec((1, D_MODEL), lambda b, i: (0, 0))
    in_specs = [pl.BlockSpec((bs, tl, D_MODEL), lambda b, i: (b, i, 0))]
    args = [u3]
    if n_lt > 1:
        per = tl // B_HALO
        in_specs.append(pl.BlockSpec((bs, B_HALO, D_MODEL), lambda b, i: (b, jnp.maximum(i * per - 1, 0), 0)))
        args.append(u3)
    in_specs += [pl.BlockSpec((bs, B_HALO, D_MODEL), lambda b, i: (b, 0, 0)),
                 pl.BlockSpec((CONV_K, D_MODEL), lambda b, i: (0, 0)),
                 vec, vec, vec,
                 pl.BlockSpec((D_MODEL, D_MODEL), lambda b, i: (0, 0)),
                 vec,
                 pl.BlockSpec((rows, D_MODEL), lambda b, i: (b * n_lt + i, 0))]
    args += [hist, wdw, bdw, lng, lnb, w2_bf, b2, res2d]
    return pl.pallas_call(
        functools.partial(_bconv_kernel, bs=bs, tl=tl, n_lt=n_lt),
        grid=(n // bs, n_lt),
        in_specs=in_specs,
        out_specs=pl.BlockSpec((rows, D_MODEL), lambda b, i: (b * n_lt + i, 0)),
        out_shape=jax.ShapeDtypeStruct((n * seq, D_MODEL), F32),
        scratch_shapes=[pltpu.VMEM((bs, B_HALO + tl, D_MODEL), F32), pltpu.VMEM((rows, D_MODEL), F32)],
        compiler_params=_params(2),
        name="b_conv_proj",
    )(*args)


def _gelu_ln_kernel(x_ref, g_ref, w_ref, lng_ref, lnb_ref, o_ref, xn_ref):
    j = pl.program_id(1)

    @pl.when(j == 0)
    def _():
        xn_ref[...] = _rms_norm(x_ref[...], g_ref[...]).astype(BF16)

    y = _dot(xn_ref[...], w_ref[...])
    y = 0.5 * y * (1.0 + lax.erf(y * (0.5 ** 0.5)))

    @pl.when(j == 0)
    def _():
        o_ref[...] = y

    @pl.when(j == 1)
    def _():
        o_ref[...] = _layer_norm(y, lng_ref[...], lnb_ref[...])


def _gelu_ln_proj(x2d, norm_g, w_bf, lng, lnb, tm):
    m = x2d.shape[0]
    vec = pl.BlockSpec((1, D_MODEL), lambda i, j: (0, 0))
    return pl.pallas_call(
        _gelu_ln_kernel,
        grid=(m // tm, 2),
        in_specs=[pl.BlockSpec((tm, D_MODEL), lambda i, j: (i, 0)), vec,
                  pl.BlockSpec((D_MODEL, D_MODEL), lambda i, j: (0, j)), vec, vec],
        out_specs=pl.BlockSpec((tm, D_MODEL), lambda i, j: (i, j)),
        out_shape=jax.ShapeDtypeStruct((m, 2 * D_MODEL), F32),
        scratch_shapes=[pltpu.VMEM((tm, D_MODEL), BF16)],
        compiler_params=_params(2),
        name="c_in_proj",
    )(x2d, norm_g, w_bf, lng, lnb)


def _cgate_kernel(u_ref, v_ref, mix_ref, brow_ref, w_ref, res_ref, o_ref, h_ref, *, r_mix, l_seq, rows):
    gw = D_MODEL // C_GROUPS
    ri = lax.broadcasted_iota(jnp.int32, (r_mix, r_mix), 0)
    ci = lax.broadcasted_iota(jnp.int32, (r_mix, r_mix), 1)
    valid = (_idiv(ri, l_seq) == _idiv(ci, l_seq)) & (_imod(ci, l_seq) <= _imod(ri, l_seq))
    for g in range(C_GROUPS):
        mg = jnp.where(valid, mix_ref[g], 0.0).astype(BF16)
        bcol = brow_ref[:, g:g + 1]
        cols = slice(g * gw, (g + 1) * gw)
        for mb in range(rows // r_mix):
            rws = slice(mb * r_mix, (mb + 1) * r_mix)
            s = _dot(mg, v_ref[rws, cols].astype(BF16)) + bcol
            h_ref[rws, cols] = (u_ref[rws, cols] * s).astype(BF16)
    o_ref[...] = res_ref[...] + _dot(h_ref[...], w_ref[...])


def _cgate(uv2d, mix, brow, w_bf, res2d, rows, l_seq):
    m = res2d.shape[0]
    r_mix = mix.shape[1]
    return pl.pallas_call(
        functools.partial(_cgate_kernel, r_mix=r_mix, l_seq=l_seq, rows=rows),
        grid=(m // rows,),
        in_specs=[pl.BlockSpec((rows, D_MODEL), lambda i: (i, 0)),
                  pl.BlockSpec((rows, D_MODEL), lambda i: (i, 1)),
                  pl.BlockSpec((C_GROUPS, r_mix, r_mix), lambda i: (0, 0, 0)),
                  pl.BlockSpec((r_mix, C_GROUPS), lambda i: (0, 0)),
                  pl.BlockSpec((D_MODEL, D_MODEL), lambda i: (0, 0)),
                  pl.BlockSpec((rows, D_MODEL), lambda i: (i, 0))],
        out_specs=pl.BlockSpec((rows, D_MODEL), lambda i: (i, 0)),
        out_shape=jax.ShapeDtypeStruct((m, D_MODEL), F32),
        scratch_shapes=[pltpu.VMEM((rows, D_MODEL), BF16)],
        compiler_params=_params(1),
        name="c_gate_proj",
    )(uv2d, uv2d, mix, brow, w_bf, res2d)


def _ffn_kernel(*refs, bs, tl, n_lt, tf):
    if n_lt > 1:
        (x_ref, xh_ref, hist_ref, g_ref, wg_ref, wu_ref, wdw_ref, bdw_ref, wd_ref,
         o_ref, tail_ref, xn_ref, hs_ref, acc_ref) = refs
    else:
        (x_ref, hist_ref, g_ref, wg_ref, wu_ref, wdw_ref, bdw_ref, wd_ref,
         o_ref, tail_ref, xn_ref, hs_ref, acc_ref) = refs
    i = pl.program_id(1)
    f = pl.program_id(2)
    n_f = pl.num_programs(2)
    rows = bs * tl
    lead = F_HALO if n_lt > 1 else 0

    @pl.when(f == 0)
    def _():
        x = x_ref[...]
        if n_lt > 1:
            xn_ref[0:F_HALO, :] = _rms_norm(xh_ref[...], g_ref[...]).astype(BF16)
        xn_ref[lead:lead + rows, :] = _rms_norm(x, g_ref[...]).astype(BF16)
        acc_ref[...] = x

    gate = _dot(xn_ref[...], wg_ref[...])
    up = _dot(xn_ref[lead:lead + rows, :], wu_ref[...])

    if n_lt > 1:
        hs_ref[0] = gate

        @pl.when(i == 0)
        def _():
            hs_ref[:, 0:F_HALO, :] = hist_ref[...]
    else:
        hs_ref[:, 0:F_HALO, :] = hist_ref[...]
        hs_ref[:, F_HALO:F_HALO + tl, :] = gate.reshape(bs, tl, tf)

    hs = hs_ref[...]
    base = F_HALO - (FFN_K - 1)
    gc = bdw_ref[...]
    for k in range(FFN_K):
        gc = gc + hs[:, base + k:base + k + tl, :] * wdw_ref[k:k + 1, :]
    act = (gc * jax.nn.sigmoid(gc)) * up.reshape(bs, tl, tf)
    acc_ref[...] += _dot(act.reshape(rows, tf).astype(BF16), wd_ref[...])
    tail_ref[...] = hs[:, tl:tl + F_HALO, :]

    @pl.when(f == n_f - 1)
    def _():
        o_ref[...] = acc_ref[...]


def _ffn(x2d, hist, norm_g, wup_bf, wdw, bdw, wdown_bf, n, seq, bs, tl, tf):
    n_lt = seq // tl
    rows = bs * tl
    n_f = D_FF // tf
    lead = F_HALO if n_lt > 1 else 0
    in_specs = [pl.BlockSpec((rows, D_MODEL), lambda b, i, f: (b * n_lt + i, 0))]
    args = [x2d]
    if n_lt > 1:
        assert bs == 1
        per = tl // F_HALO
        in_specs.append(pl.BlockSpec((F_HALO, D_MODEL),
                                     lambda b, i, f: (jnp.maximum((b * n_lt + i) * per - 1, 0), 0)))
        args.append(x2d)
    in_specs += [pl.BlockSpec((bs, F_HALO, tf), lambda b, i, f: (b, 0, f)),
                 pl.BlockSpec((1, D_MODEL), lambda b, i, f: (0, 0)),
                 pl.BlockSpec((D_MODEL, tf), lambda b, i, f: (0, f)),
                 pl.BlockSpec((D_MODEL, tf), lambda b, i, f: (0, f + n_f)),
                 pl.BlockSpec((FFN_K, tf), lambda b, i, f: (0, f)),
                 pl.BlockSpec((1, tf), lambda b, i, f: (0, f)),
                 pl.BlockSpec((tf, D_MODEL), lambda b, i, f: (f, 0))]
    args += [hist, norm_g, wup_bf, wup_bf, wdw, bdw, wdown_bf]
    y, tail = pl.pallas_call(
        functools.partial(_ffn_kernel, bs=bs, tl=tl, n_lt=n_lt, tf=tf),
        grid=(n // bs, n_lt, n_f),
        in_specs=in_specs,
        out_specs=[pl.BlockSpec((rows, D_MODEL), lambda b, i, f: (b * n_lt + i, 0)),
                   pl.BlockSpec((bs, F_HALO, tf), lambda b, i, f: (b * n_lt + i, 0, f))],
        out_shape=[jax.ShapeDtypeStruct((n * seq, D_MODEL), F32),
                   jax.ShapeDtypeStruct((n * n_lt, F_HALO, D_FF), F32)],
        scratch_shapes=[pltpu.VMEM((lead + rows, D_MODEL), BF16),
                        pltpu.VMEM((bs, F_HALO + tl, tf), F32),
                        pltpu.VMEM((rows, D_MODEL), F32)],
        compiler_params=_params(3),
        name="conv_ffn",
    )(*args)
    tail = tail.reshape(n, n_lt, F_HALO, D_FF)[:, n_lt - 1, F_HALO - (FFN_K - 1):, :]
    return y, tail


def _front_pad(a, rows):
    return jnp.pad(a, ((0, 0), (rows - a.shape[1], 0), (0, 0)))


def _trunk(x, a_cache, b_cache, f_cache, W, sample):
    n, seq, _ = x.shape
    m = n * seq
    tm = 256 if sample else 1024
    x2 = x.reshape(m, D_MODEL)
    wid = H_G * DH
    ia = ib = ic = 0
    new_a = [[] for _ in range(N_GROUPS)]
    new_b, new_c, new_f = [], [], []
    for layer in range(DEPTH):
        kind = layer % 3
        if kind == 0:
            gains = jnp.stack([W['a_q_gain'][ia], W['a_k_gain'][ia],
                               jnp.ones_like(W['a_q_gain'][ia])], axis=1).reshape(N_GROUPS * 3, 1, DH)
            qkv = _qkv_proj(x2, W['a_norm'][ia][None], W['a_w_in'][ia], gains, tm)
            qkv3 = qkv.reshape(n, seq, N_GROUPS * 3 * wid)
            if not sample:
                att = _attn_prompt(qkv3)
                x2 = _outproj(att.reshape(m, wid), W['a_w_out'][ia], x2, tm)
                for g, (win, _) in enumerate(A_GROUPS):
                    keep = min(win, seq)
                    kv = qkv3[:, seq - keep:, g * 3 * wid + wid:(g + 1) * 3 * wid]
                    new_a[g].append(kv.reshape(n, keep, 2, H_G, DH))
            else:
                parts = []
                for g in range(N_GROUPS):
                    o_g, lse_g, c_new = _attn_sample(qkv3, a_cache[g], ia, g)
                    parts += [o_g.reshape(m, wid), lse_g.reshape(m, wid)]
                    new_a[g].append(c_new.reshape(n, c_new.shape[1], 2, H_G, DH))
                x2 = _merge_outproj(parts, W['a_w_out'][ia], x2)
            ia += 1
        elif kind == 1:
            u = _glu_proj(x2, W['b_norm'][ib][None], W['b_w_pw1'][ib], W['b_b_pw1'][ib][None], tm, 512)
            u3 = u.reshape(n, seq, D_MODEL)
            buf = b_cache[ib] if sample else jnp.zeros((n, CONV_K - 1, D_MODEL), F32)
            hist = _front_pad(buf, B_HALO)
            bs, tl = (n, seq) if sample else (1, 512)
            x2 = _bconv(u3, hist, W['b_w_dw'][ib], W['b_b_dw'][ib][None], W['b_ln_g'][ib][None],
                        W['b_ln_b'][ib][None], W['b_w_pw2'][ib], W['b_b_pw2'][ib][None], x2, bs, tl)
            cat = jnp.concatenate([buf, u3], axis=1)
            new_b.append(cat[:, cat.shape[1] - (CONV_K - 1):])
            ib += 1
        else:
            uv = _gelu_ln_proj(x2, W['c_norm'][ic][None], W['c_w_in'][ic], W['c_ln_g'][ic][None],
                               W['c_ln_b'][ic][None], tm)
            if sample:
                mix = jnp.tile(W['c_w_s'][ic][:, :seq, :seq], (1, n, n))
                brow = jnp.tile(W['c_b_s'][ic].T[:seq], (n, 1))
                rows, l_seq = m, seq
                new_c.append(uv[:, D_MODEL:].reshape(n, seq, D_MODEL))
            else:
                mix = W['c_w_s'][ic]
                brow = W['c_b_s'][ic].T
                rows, l_seq = 512, CHUNK
            x2 = _cgate(uv, mix, brow, W['c_w_out'][ic], x2, rows, l_seq)
            ic += 1
        buf = f_cache[layer] if sample else jnp.zeros((n, FFN_K - 1, D_FF), F32)
        hist = _front_pad(buf, F_HALO)
        bs, tl = (n, seq) if sample else (1, 1024)
        x2, tail = _ffn(x2, hist, W['f_norm'][layer][None], W['f_w_up'][layer], W['f_w_dw'][layer],
                        W['f_b_dw'][layer][None], W['f_w_down'][layer], n, seq, bs, tl, 256)
        new_f.append(tail)
    out = [x2.reshape(n, seq, D_MODEL)] + [jnp.stack(a, axis=0) for a in new_a] + [jnp.stack(new_b, axis=0)]
    if sample:
        out.append(jnp.stack(new_c, axis=0))
    out.append(jnp.stack(new_f, axis=0))
    return out


def kernel(x_prompt, x_sample, cache_a_g0_kv, cache_a_g1_kv, cache_a_g2_kv, state_b_conv, state_ffn_conv,
           a_norm, a_w_in, a_q_gain, a_k_gain, a_w_out,
           b_norm, b_w_pw1, b_b_pw1, b_w_dw, b_b_dw, b_ln_g, b_ln_b, b_w_pw2, b_b_pw2,
           c_norm, c_w_in, c_ln_g, c_ln_b, c_w_s, c_b_s, c_w_out,
           f_norm, f_w_up, f_w_dw, f_b_dw, f_w_down):
    W = dict(a_norm=a_norm, a_w_in=a_w_in.astype(BF16), a_q_gain=a_q_gain, a_k_gain=a_k_gain,
             a_w_out=a_w_out.astype(BF16),
             b_norm=b_norm, b_w_pw1=b_w_pw1.astype(BF16), b_b_pw1=b_b_pw1, b_w_dw=b_w_dw, b_b_dw=b_b_dw,
             b_ln_g=b_ln_g, b_ln_b=b_ln_b, b_w_pw2=b_w_pw2.astype(BF16), b_b_pw2=b_b_pw2,
             c_norm=c_norm, c_w_in=c_w_in.astype(BF16), c_ln_g=c_ln_g, c_ln_b=c_ln_b, c_w_s=c_w_s, c_b_s=c_b_s,
             c_w_out=c_w_out.astype(BF16),
             f_norm=f_norm, f_w_up=f_w_up.astype(BF16), f_w_dw=f_w_dw, f_b_dw=f_b_dw,
             f_w_down=f_w_down.astype(BF16))
    yp, pa0, pa1, pa2, pb, pf = _trunk(x_prompt, None, None, None, W, sample=False)
    caches = [c.reshape(c.shape[0], c.shape[1], c.shape[2], 2 * H_G * DH)
              for c in (cache_a_g0_kv, cache_a_g1_kv, cache_a_g2_kv)]
    ys, sa0, sa1, sa2, sb, sc, sf = _trunk(x_sample, caches, state_b_conv, state_ffn_conv, W, sample=True)
    return (yp, ys, pa0, pa1, pa2, pb, pf, sa0, sa1, sa2, sb, sc, sf)
```

```python
import functools

import jax
import jax.numpy as jnp
from jax import lax
from jax.experimental import pallas as pl
from jax.experimental.pallas import tpu as pltpu

F32 = jnp.float32
BF16 = jnp.bfloat16

D_MODEL = 1024
DEPTH = 4
A_GROUPS = ((128, 1), (512, 4), (2048, 16))
N_GROUPS = 3
H_G = 8
DH = 128
ATT_BLK = 128
CONV_K = 31
C_GROUPS = 8
CHUNK = 128
D_FF = 2816
FFN_K = 3
EPS = 1e-6
NEG = -1e30

LANES = 128
F32_SUBLANES = 8
BF16_SUBLANES = 16
VMEM_LIMIT_BYTES = 56 * 1024 * 1024

B_HALO = 32
F_HALO = 16


def _params(n_axes):
    return pltpu.CompilerParams(dimension_semantics=("arbitrary",) * n_axes,
                                vmem_limit_bytes=VMEM_LIMIT_BYTES)


def _rms_norm(x, g):
    return x * lax.rsqrt(jnp.mean(x * x, axis=-1, keepdims=True) + EPS) * g


def _layer_norm(x, g, b):
    mu = jnp.mean(x, axis=-1, keepdims=True)
    xc = x - mu
    return xc * lax.rsqrt(jnp.mean(xc * xc, axis=-1, keepdims=True) + EPS) * g + b


def _idiv(x, n):
    assert n & (n - 1) == 0
    return jnp.right_shift(x, n.bit_length() - 1)


def _imod(x, n):
    assert n & (n - 1) == 0
    return jnp.bitwise_and(x, n - 1)


def _dot(a, b):
    return jnp.dot(a, b, preferred_element_type=F32)


def _dot_nt(a, b):
    return lax.dot_general(a, b, (((1,), (1,)), ((), ())), preferred_element_type=F32)


KV_ROWS = 2 * H_G


def _qkv_kernel(*refs, tm, n_t, keeps, n_alias):
    x_ref, g_ref, w_ref, gain_ref = refs[:4]
    o_ref = refs[4 + n_alias]
    cache_refs = refs[5 + n_alias:5 + n_alias + len(keeps)]
    xn_ref = refs[-1]
    i = pl.program_id(0)
    j = pl.program_id(1)

    @pl.when(j == 0)
    def _():
        xn_ref[...] = _rms_norm(x_ref[...], g_ref[...]).astype(BF16)

    y = _dot(xn_ref[...], w_ref[...])
    sec = j % 3
    grp = j // 3

    def emit(kv, heads):
        for g, c_ref in enumerate(cache_refs):
            keep = keeps[g]
            n_rows = min(keep, tm)
            tiles = max(keep // tm, 1)

            @pl.when((grp == g) & (i % n_t >= n_t - tiles))
            def _():
                for h in range(H_G):
                    c_ref[pl.ds(kv * H_G + h, n_rows, stride=KV_ROWS), :] = heads[h][tm - n_rows:tm, :]

    @pl.when(sec < 2)
    def _():
        gain = gain_ref[0]
        heads = []
        for h in range(H_G):
            yh = y[:, h * DH:(h + 1) * DH]
            ms = jnp.mean(yh * yh, axis=-1, keepdims=True)
            yn = yh * lax.rsqrt(ms + EPS) * gain
            o_ref[:, h * DH:(h + 1) * DH] = yn
            heads.append(yn)
        if cache_refs:
            @pl.when(sec == 1)
            def _():
                emit(0, heads)

    @pl.when(sec == 2)
    def _():
        o_ref[...] = y
        if cache_refs:
            emit(1, [y[:, h * DH:(h + 1) * DH] for h in range(H_G)])


def _qkv_proj(x2d, norm_g, w_bf, gains, tm, seq=None, layer=0, caches=None, n_layers=0):
    m = x2d.shape[0]
    n_blk = N_GROUPS * 3
    wid = H_G * DH
    in_specs = [pl.BlockSpec((tm, D_MODEL), lambda i, j: (i, 0)),
                pl.BlockSpec((1, D_MODEL), lambda i, j: (0, 0)),
                pl.BlockSpec((D_MODEL, wid), lambda i, j: (0, j)),
                pl.BlockSpec((1, 1, DH), lambda i, j: (j, 0, 0))]
    out_specs = [pl.BlockSpec((tm, wid), lambda i, j: (i, j))]
    out_shape = [jax.ShapeDtypeStruct((m, n_blk * wid), F32)]
    args = [x2d, norm_g, w_bf, gains]
    keeps, aliases, n_t = (), {}, 1
    if seq is not None:
        n_t = seq // tm
        keeps = tuple(min(win, seq) for win, _ in A_GROUPS)
        for g, keep in enumerate(keeps):
            n_rows = min(keep, tm)
            tiles = max(keep // tm, 1)
            out_specs.append(pl.BlockSpec(
                (None, None, n_rows * KV_ROWS, DH),
                lambda i, j, tiles=tiles: (layer, i // n_t, jnp.maximum(i % n_t - (n_t - tiles), 0), 0)))
            out_shape.append(jax.ShapeDtypeStruct((n_layers, m // seq, keep * KV_ROWS, DH), F32))
        if caches is not None:
            for g, c in enumerate(caches):
                in_specs.append(pl.BlockSpec(memory_space=pl.ANY))
                args.append(c)
                aliases[4 + g] = 1 + g
    outs = pl.pallas_call(
        functools.partial(_qkv_kernel, tm=tm, n_t=n_t, keeps=keeps, n_alias=len(aliases)),
        grid=(m // tm, n_blk),
        in_specs=in_specs,
        out_specs=out_specs,
        out_shape=out_shape,
        input_output_aliases=aliases,
        scratch_shapes=[pltpu.VMEM((tm, D_MODEL), BF16)],
        compiler_params=_params(2),
        name="qkv_proj",
    )(*args)
    return outs[0], list(outs[1:])


def _attn_prompt_kernel(q0, k0, v0, q1, k1, v1, q2, k2, v2, o_ref, acc_ref, m_ref, l_ref, *, seq):
    scale = DH ** -0.5
    blk = ATT_BLK
    a_i = lax.broadcasted_iota(jnp.int32, (blk, 2 * blk), 0)
    c_i = lax.broadcasted_iota(jnp.int32, (blk, 2 * blk), 1)
    mask_two = ((c_i >= blk) & ((c_i - blk) <= a_i)) | ((c_i < blk) & (c_i >= a_i))
    a_1 = lax.broadcasted_iota(jnp.int32, (blk, blk), 0)
    c_1 = lax.broadcasted_iota(jnp.int32, (blk, blk), 1)
    mask_one = c_1 <= a_1

    groups = ((q0, k0, v0), (q1, k1, v1), (q2, k2, v2))
    for g, (q_ref, k_ref, v_ref) in enumerate(groups):
        dil = A_GROUPS[g][1]
        n_sub = seq // dil
        n_blk = n_sub // blk
        for r in range(dil):
            for i in range(n_blk):
                def rows(first_blk, n):
                    start = r + first_blk * blk * dil
                    if dil == 1:
                        return pl.ds(start, n)
                    return pl.ds(start, n, stride=dil)

                rq = rows(i, blk)
                q = q_ref[0, rq, :].astype(BF16)
                if i == 0:
                    rk, mask = rq, mask_one
                else:
                    rk, mask = rows(i - 1, 2 * blk), mask_two
                k = k_ref[0, rk, :].astype(BF16)
                v = v_ref[0, rk, :].astype(BF16)
                s = _dot_nt(q, k) * scale
                s = jnp.where(mask, s, NEG)
                m_blk = jnp.max(s, axis=-1, keepdims=True)
                p = jnp.exp(s - m_blk)
                acc_ref[g, rq, :] = _dot(p.astype(BF16), v)
                m_ref[g, rq, :] = jnp.broadcast_to(m_blk, (blk, LANES))
                l_ref[g, rq, :] = jnp.broadcast_to(jnp.sum(p, axis=-1, keepdims=True), (blk, LANES))
    m_all = jnp.maximum(jnp.maximum(m_ref[0], m_ref[1]), m_ref[2])
    num = jnp.zeros((seq, DH), F32)
    den = jnp.zeros((seq, LANES), F32)
    for g in range(N_GROUPS):
        w = jnp.exp(m_ref[g] - m_all)
        num = num + w * acc_ref[g]
        den = den + w * l_ref[g]
    o_ref[0] = (num / den).astype(o_ref.dtype)


def _attn_prompt(qkv3):
    nb, seq, _ = qkv3.shape
    in_specs = []
    for g in range(N_GROUPS):
        for sec in range(3):
            col = (g * 3 + sec) * H_G
            in_specs.append(pl.BlockSpec((1, seq, DH), lambda b, h, c=col: (b, 0, c + h)))
    return pl.pallas_call(
        functools.partial(_attn_prompt_kernel, seq=seq),
        grid=(nb, H_G),
        in_specs=in_specs,
        out_specs=pl.BlockSpec((1, seq, DH), lambda b, h: (b, 0, h)),
        out_shape=jax.ShapeDtypeStruct((nb, seq, H_G * DH), BF16),
        scratch_shapes=[pltpu.VMEM((N_GROUPS, seq, DH), F32), pltpu.VMEM((N_GROUPS, seq, LANES), F32),
                        pltpu.VMEM((N_GROUPS, seq, LANES), F32)],
        compiler_params=_params(2),
        name="attn_prompt",
    )(*([qkv3] * 9))


def _attn_sample_kernel(*refs, dil, band, n_buf, tc, n_c, t_new, n_alias):
    q_ref, kn_ref, vn_ref, c_ref = refs[:4]
    nxt_ref = refs[4] if n_c > 1 else None
    n_in = (5 if n_c > 1 else 4) + n_alias
    o_ref, lse_ref, cout_ref, m_ref, l_ref, acc_ref = refs[n_in:]
    c = pl.program_id(1)
    scale = DH ** -0.5
    wid = H_G * DH
    n_q = H_G * t_new

    q = q_ref[0]
    qt = jnp.concatenate([q] * H_G, axis=0)
    row_h = _idiv(lax.broadcasted_iota(jnp.int32, (n_q, wid), 0), t_new)
    col_h = _idiv(lax.broadcasted_iota(jnp.int32, (n_q, wid), 1), DH)
    qbd = jnp.where(row_h == col_h, qt, 0.0).astype(BF16)

    @pl.when(c == 0)
    def _():
        m_ref[...] = jnp.full(m_ref.shape, NEG, F32)
        l_ref[...] = jnp.zeros(l_ref.shape, F32)
        acc_ref[...] = jnp.zeros(acc_ref.shape, F32)

    def update(keys, vals, idx):
        t = _imod(lax.broadcasted_iota(jnp.int32, idx.shape, 0), t_new)
        back = n_buf + t - idx
        valid = (back >= 0) & (back <= dil * band) & (_imod(back, dil) == 0)
        s = jnp.where(valid, _dot_nt(qbd, keys) * scale, NEG)
        m_prev = m_ref[...]
        m_new = jnp.maximum(m_prev, jnp.max(s, axis=-1, keepdims=True))
        alpha = jnp.exp(m_prev - m_new)
        p = jnp.where(valid, jnp.exp(s - m_new), 0.0)
        l_ref[...] = alpha * l_ref[...] + jnp.sum(p, axis=-1, keepdims=True)
        acc_ref[...] = alpha * acc_ref[...] + _dot(p.astype(BF16), vals)
        m_ref[...] = m_new

    def all_heads(kv):
        return jnp.concatenate([c_ref[0, pl.ds(kv * H_G + h, tc, stride=KV_ROWS), :] for h in range(H_G)],
                               axis=1).astype(BF16)

    idx_c = c * tc + lax.broadcasted_iota(jnp.int32, (n_q, tc), 1)
    update(all_heads(0), all_heads(1), idx_c)

    body = (tc - t_new) * KV_ROWS
    cout_ref[0, 0:body, :] = c_ref[0, t_new * KV_ROWS:tc * KV_ROWS, :]
    last = c == n_c - 1

    if n_c > 1:
        @pl.when(jnp.logical_not(last))
        def _():
            cout_ref[0, body:tc * KV_ROWS, :] = nxt_ref[0]

    @pl.when(last)
    def _():
        kn = kn_ref[0]
        vn = vn_ref[0]
        for h in range(H_G):
            cout_ref[0, pl.ds(body + h, t_new, stride=KV_ROWS), :] = kn[:, h * DH:(h + 1) * DH]
            cout_ref[0, pl.ds(body + H_G + h, t_new, stride=KV_ROWS), :] = vn[:, h * DH:(h + 1) * DH]
        pad = jnp.zeros((LANES - t_new, wid), F32)
        idx_n = n_buf + lax.broadcasted_iota(jnp.int32, (n_q, LANES), 1)
        update(jnp.concatenate([kn, pad], axis=0).astype(BF16),
               jnp.concatenate([vn, pad], axis=0).astype(BF16), idx_n)
        l = l_ref[...]
        o_full = acc_ref[...] / l
        lse = m_ref[...] + jnp.log(l)
        for h in range(H_G):
            o_ref[0, :, h * DH:(h + 1) * DH] = o_full[h * t_new:(h + 1) * t_new, h * DH:(h + 1) * DH]
            lse_ref[0, :, h * DH:(h + 1) * DH] = jnp.broadcast_to(lse[h * t_new:(h + 1) * t_new], (t_new, DH))


def _attn_sample(qkv3, cache4, layer, g, new_cache):
    ndb, t_new, _ = qkv3.shape
    n_layers = cache4.shape[0]
    n_buf = cache4.shape[2] // KV_ROWS
    win, dil = A_GROUPS[g]
    band = win // dil
    wid = H_G * DH
    tc = min(n_buf, 512)
    n_c = n_buf // tc
    in_specs = [pl.BlockSpec((1, t_new, wid), lambda b, c: (b, 0, g * 3)),
                pl.BlockSpec((1, t_new, wid), lambda b, c: (b, 0, g * 3 + 1)),
                pl.BlockSpec((1, t_new, wid), lambda b, c: (b, 0, g * 3 + 2)),
                pl.BlockSpec((None, 1, tc * KV_ROWS, DH), lambda b, c: (layer, b, c, 0))]
    args = [qkv3, qkv3, qkv3, cache4]
    if n_c > 1:
        per = tc // t_new
        last_blk = n_buf // t_new - 1
        in_specs.append(pl.BlockSpec((None, 1, t_new * KV_ROWS, DH),
                                     lambda b, c: (layer, b, jnp.minimum((c + 1) * per, last_blk), 0)))
        args.append(cache4)
    aliases = {}
    if new_cache is not None:
        aliases[len(args)] = 2
        in_specs.append(pl.BlockSpec(memory_space=pl.ANY))
        args.append(new_cache)
    n_q = H_G * t_new
    return pl.pallas_call(
        functools.partial(_attn_sample_kernel, dil=dil, band=band, n_buf=n_buf, tc=tc, n_c=n_c, t_new=t_new,
                          n_alias=len(aliases)),
        grid=(ndb, n_c),
        in_specs=in_specs,
        out_specs=[pl.BlockSpec((1, t_new, wid), lambda b, c: (b, 0, 0)),
                   pl.BlockSpec((1, t_new, wid), lambda b, c: (b, 0, 0)),
                   pl.BlockSpec((None, 1, tc * KV_ROWS, DH), lambda b, c: (layer, b, c, 0))],
        out_shape=[jax.ShapeDtypeStruct((ndb, t_new, wid), F32),
                   jax.ShapeDtypeStruct((ndb, t_new, wid), F32),
                   jax.ShapeDtypeStruct((n_layers, ndb, n_buf * KV_ROWS, DH), F32)],
        input_output_aliases=aliases,
        scratch_shapes=[pltpu.VMEM((n_q, 1), F32), pltpu.VMEM((n_q, 1), F32), pltpu.VMEM((n_q, wid), F32)],
        compiler_params=_params(2),
        name="attn_sample_g%d" % g,
    )(*args)


def _outproj_kernel(h_ref, w_ref, res_ref, o_ref):
    o_ref[...] = res_ref[...] + _dot(h_ref[...], w_ref[...])


def _outproj(h2d, w_bf, res2d, tm):
    m = h2d.shape[0]
    return pl.pallas_call(
        _outproj_kernel,
        grid=(m // tm,),
        in_specs=[pl.BlockSpec((tm, D_MODEL), lambda i: (i, 0)),
                  pl.BlockSpec((D_MODEL, D_MODEL), lambda i: (0, 0)),
                  pl.BlockSpec((tm, D_MODEL), lambda i: (i, 0))],
        out_specs=pl.BlockSpec((tm, D_MODEL), lambda i: (i, 0)),
        out_shape=jax.ShapeDtypeStruct((m, D_MODEL), F32),
        compiler_params=_params(1),
        name="a_outproj",
    )(h2d, w_bf, res2d)


def _merge_outproj_kernel(o0, l0, o1, l1, o2, l2, w_ref, res_ref, o_ref):
    la, lb, lc = l0[...], l1[...], l2[...]
    mx = jnp.maximum(jnp.maximum(la, lb), lc)
    ea, eb, ec = jnp.exp(la - mx), jnp.exp(lb - mx), jnp.exp(lc - mx)
    att = (ea * o0[...] + eb * o1[...] + ec * o2[...]) / (ea + eb + ec)
    o_ref[...] = res_ref[...] + _dot(att.astype(BF16), w_ref[...])


def _merge_outproj(parts, w_bf, res2d):
    m = res2d.shape[0]
    row = pl.BlockSpec((m, D_MODEL), lambda i: (0, 0))
    return pl.pallas_call(
        _merge_outproj_kernel,
        grid=(1,),
        in_specs=[row] * 6 + [pl.BlockSpec((D_MODEL, D_MODEL), lambda i: (0, 0)), row],
        out_specs=row,
        out_shape=jax.ShapeDtypeStruct((m, D_MODEL), F32),
        compiler_params=_params(1),
        name="a_merge_outproj",
    )(*parts, w_bf, res2d)


def _glu_kernel(x_ref, g_ref, wa_ref, wb_ref, ba_ref, bb_ref, o_ref, xn_ref):
    @pl.when(pl.program_id(1) == 0)
    def _():
        xn_ref[...] = _rms_norm(x_ref[...], g_ref[...]).astype(BF16)

    xn = xn_ref[...]
    a = _dot(xn, wa_ref[...]) + ba_ref[...]
    b = _dot(xn, wb_ref[...]) + bb_ref[...]
    o_ref[...] = a * jax.nn.sigmoid(b)


def _glu_proj(x2d, norm_g, w_bf, bias, tm, tn):
    m = x2d.shape[0]
    nb = D_MODEL // tn
    return pl.pallas_call(
        _glu_kernel,
        grid=(m // tm, nb),
        in_specs=[pl.BlockSpec((tm, D_MODEL), lambda i, j: (i, 0)),
                  pl.BlockSpec((1, D_MODEL), lambda i, j: (0, 0)),
                  pl.BlockSpec((D_MODEL, tn), lambda i, j: (0, j)),
                  pl.BlockSpec((D_MODEL, tn), lambda i, j: (0, j + nb)),
                  pl.BlockSpec((1, tn), lambda i, j: (0, j)),
                  pl.BlockSpec((1, tn), lambda i, j: (0, j + nb))],
        out_specs=pl.BlockSpec((tm, tn), lambda i, j: (i, j)),
        out_shape=jax.ShapeDtypeStruct((m, D_MODEL), F32),
        scratch_shapes=[pltpu.VMEM((tm, D_MODEL), BF16)],
        compiler_params=_params(2),
        name="b_glu_proj",
    )(x2d, norm_g, w_bf, w_bf, bias, bias)


def _bconv_kernel(*refs, bs, tl, n_lt):
    if n_lt > 1:
        (u_ref, halo_ref, hist_ref, wdw_ref, bdw_ref, lng_ref, lnb_ref, w2_ref, b2_ref, res_ref,
         o_ref, cat_ref, y_ref) = refs
    else:
        (u_ref, hist_ref, wdw_ref, bdw_ref, lng_ref, lnb_ref, w2_ref, b2_ref, res_ref,
         o_ref, cat_ref, y_ref) = refs
    if n_lt > 1:
        first = pl.program_id(1) == 0

        @pl.when(first)
        def _():
            cat_ref[:, 0:B_HALO, :] = hist_ref[...]

        @pl.when(jnp.logical_not(first))
        def _():
            cat_ref[:, 0:B_HALO, :] = halo_ref[...]
    else:
        cat_ref[:, 0:B_HALO, :] = hist_ref[...]
    cat_ref[:, B_HALO:B_HALO + tl, :] = u_ref[...]

    rc = min(tl, 32)
    lc = LANES
    n_rc = tl // rc
    lead = B_HALO - (CONV_K - 1)
    sub = F32_SUBLANES

    def body(it, carry):
        b = it // n_rc
        r0 = pl.multiple_of((it % n_rc) * rc, sub)
        pieces = []
        for c0 in range(0, D_MODEL, lc):
            win = cat_ref[b, pl.ds(r0, rc + B_HALO), c0:c0 + lc]
            shifted = [win] + [win[s:s + rc + B_HALO - sub, :] for s in range(1, sub)]
            acc = jnp.broadcast_to(bdw_ref[:, c0:c0 + lc], (rc, lc))
            for k in range(CONV_K):
                q, s = divmod(lead + k, sub)
                acc = acc + shifted[s][q * sub:q * sub + rc, :] * wdw_ref[k:k + 1, c0:c0 + lc]
            pieces.append(acc)
        conv = jnp.concatenate(pieces, axis=1)
        y = _layer_norm(conv, lng_ref[...], lnb_ref[...])
        y_ref[pl.ds(pl.multiple_of(b * tl + r0, F32_SUBLANES), rc), :] = y * jax.nn.sigmoid(y)
        return carry

    lax.fori_loop(0, bs * n_rc, body, 0)
    o_ref[...] = res_ref[...] + _dot(y_ref[...].astype(BF16), w2_ref[...]) + b2_ref[...]


def _bconv(u3, hist, wdw, bdw, lng, lnb, w2_bf, b2, res2d, bs, tl):
    n, seq, _ = u3.shape
    n_lt = seq // tl
    rows = bs * tl
    vec = pl.BlockSpec((1, D_MODEL), lambda b, i: (0, 0))
    in_specs = [pl.BlockSpec((bs, tl, D_MODEL), lambda b, i: (b, i, 0))]
    args = [u3]
    if n_lt > 1:
        per = tl // B_HALO
        in_specs.append(pl.BlockSpec((bs, B_HALO, D_MODEL), lambda b, i: (b, jnp.maximum(i * per - 1, 0), 0)))
        args.append(u3)
    in_specs += [pl.BlockSpec((bs, B_HALO, D_MODEL), lambda b, i: (b, 0, 0)),
                 pl.BlockSpec((CONV_K, D_MODEL), lambda b, i: (0, 0)),
                 vec, vec, vec,
                 pl.BlockSpec((D_MODEL, D_MODEL), lambda b, i: (0, 0)),
                 vec,
                 pl.BlockSpec((rows, D_MODEL), lambda b, i: (b * n_lt + i, 0))]
    args += [hist, wdw, bdw, lng, lnb, w2_bf, b2, res2d]
    return pl.pallas_call(
        functools.partial(_bconv_kernel, bs=bs, tl=tl, n_lt=n_lt),
        grid=(n // bs, n_lt),
        in_specs=in_specs,
        out_specs=pl.BlockSpec((rows, D_MODEL), lambda b, i: (b * n_lt + i, 0)),
        out_shape=jax.ShapeDtypeStruct((n * seq, D_MODEL), F32),
        scratch_shapes=[pltpu.VMEM((bs, B_HALO + tl, D_MODEL), F32), pltpu.VMEM((rows, D_MODEL), F32)],
        compiler_params=_params(2),
        name="b_conv_proj",
    )(*args)


def _gelu_ln_kernel(x_ref, g_ref, w_ref, lng_ref, lnb_ref, o_ref, xn_ref):
    j = pl.program_id(1)

    @pl.when(j == 0)
    def _():
        xn_ref[...] = _rms_norm(x_ref[...], g_ref[...]).astype(BF16)

    y = _dot(xn_ref[...], w_ref[...])
    y = 0.5 * y * (1.0 + lax.erf(y * (0.5 ** 0.5)))

    @pl.when(j == 0)
    def _():
        o_ref[...] = y

    @pl.when(j == 1)
    def _():
        o_ref[...] = _layer_norm(y, lng_ref[...], lnb_ref[...])


def _gelu_ln_proj(x2d, norm_g, w_bf, lng, lnb, tm):
    m = x2d.shape[0]
    vec = pl.BlockSpec((1, D_MODEL), lambda i, j: (0, 0))
    return pl.pallas_call(
        _gelu_ln_kernel,
        grid=(m // tm, 2),
        in_specs=[pl.BlockSpec((tm, D_MODEL), lambda i, j: (i, 0)), vec,
                  pl.BlockSpec((D_MODEL, D_MODEL), lambda i, j: (0, j)), vec, vec],
        out_specs=pl.BlockSpec((tm, D_MODEL), lambda i, j: (i, j)),
        out_shape=jax.ShapeDtypeStruct((m, 2 * D_MODEL), F32),
        scratch_shapes=[pltpu.VMEM((tm, D_MODEL), BF16)],
        compiler_params=_params(2),
        name="c_in_proj",
    )(x2d, norm_g, w_bf, lng, lnb)


def _cgate_kernel(u_ref, v_ref, mix_ref, brow_ref, w_ref, res_ref, o_ref, h_ref, *, r_mix, l_seq, rows):
    gw = D_MODEL // C_GROUPS
    ri = lax.broadcasted_iota(jnp.int32, (r_mix, r_mix), 0)
    ci = lax.broadcasted_iota(jnp.int32, (r_mix, r_mix), 1)
    valid = (_idiv(ri, l_seq) == _idiv(ci, l_seq)) & (_imod(ci, l_seq) <= _imod(ri, l_seq))
    for g in range(C_GROUPS):
        mg = jnp.where(valid, mix_ref[g], 0.0).astype(BF16)
        bcol = brow_ref[:, g:g + 1]
        cols = slice(g * gw, (g + 1) * gw)
        for mb in range(rows // r_mix):
            rws = slice(mb * r_mix, (mb + 1) * r_mix)
            s = _dot(mg, v_ref[rws, cols].astype(BF16)) + bcol
            h_ref[rws, cols] = (u_ref[rws, cols] * s).astype(BF16)
    o_ref[...] = res_ref[...] + _dot(h_ref[...], w_ref[...])


def _cgate(uv2d, mix, brow, w_bf, res2d, rows, l_seq):
    m = res2d.shape[0]
    r_mix = mix.shape[1]
    return pl.pallas_call(
        functools.partial(_cgate_kernel, r_mix=r_mix, l_seq=l_seq, rows=rows),
        grid=(m // rows,),
        in_specs=[pl.BlockSpec((rows, D_MODEL), lambda i: (i, 0)),
                  pl.BlockSpec((rows, D_MODEL), lambda i: (i, 1)),
                  pl.BlockSpec((C_GROUPS, r_mix, r_mix), lambda i: (0, 0, 0)),
                  pl.BlockSpec((r_mix, C_GROUPS), lambda i: (0, 0)),
                  pl.BlockSpec((D_MODEL, D_MODEL), lambda i: (0, 0)),
                  pl.BlockSpec((rows, D_MODEL), lambda i: (i, 0))],
        out_specs=pl.BlockSpec((rows, D_MODEL), lambda i: (i, 0)),
        out_shape=jax.ShapeDtypeStruct((m, D_MODEL), F32),
        scratch_shapes=[pltpu.VMEM((rows, D_MODEL), BF16)],
        compiler_params=_params(1),
        name="c_gate_proj",
    )(uv2d, uv2d, mix, brow, w_bf, res2d)


def _ffn_kernel(*refs, bs, tl, n_lt, tf):
    if n_lt > 1:
        (x_ref, xh_ref, hist_ref, g_ref, wg_ref, wu_ref, wdw_ref, bdw_ref, wd_ref,
         o_ref, tail_ref, xn_ref, hs_ref, acc_ref) = refs
    else:
        (x_ref, hist_ref, g_ref, wg_ref, wu_ref, wdw_ref, bdw_ref, wd_ref,
         o_ref, tail_ref, xn_ref, hs_ref, acc_ref) = refs
    i = pl.program_id(1)
    f = pl.program_id(2)
    n_f = pl.num_programs(2)
    rows = bs * tl
    lead = F_HALO if n_lt > 1 else 0

    @pl.when(f == 0)
    def _():
        x = x_ref[...]
        if n_lt > 1:
            xn_ref[0:F_HALO, :] = _rms_norm(xh_ref[...], g_ref[...]).astype(BF16)
        xn_ref[lead:lead + rows, :] = _rms_norm(x, g_ref[...]).astype(BF16)
        acc_ref[...] = x

    gate = _dot(xn_ref[...], wg_ref[...])
    up = _dot(xn_ref[lead:lead + rows, :], wu_ref[...])

    if n_lt > 1:
        hs_ref[0] = gate

        @pl.when(i == 0)
        def _():
            hs_ref[:, 0:F_HALO, :] = hist_ref[...]
    else:
        hs_ref[:, 0:F_HALO, :] = hist_ref[...]
        hs_ref[:, F_HALO:F_HALO + tl, :] = gate.reshape(bs, tl, tf)

    hs = hs_ref[...]
    base = F_HALO - (FFN_K - 1)
    gc = bdw_ref[...]
    for k in range(FFN_K):
        gc = gc + hs[:, base + k:base + k + tl, :] * wdw_ref[k:k + 1, :]
    act = (gc * jax.nn.sigmoid(gc)) * up.reshape(bs, tl, tf)
    acc_ref[...] += _dot(act.reshape(rows, tf).astype(BF16), wd_ref[...])
    tail_ref[...] = hs[:, tl:tl + F_HALO, :]

    @pl.when(f == n_f - 1)
    def _():
        o_ref[...] = acc_ref[...]


def _ffn(x2d, hist, norm_g, wup_bf, wdw, bdw, wdown_bf, n, seq, bs, tl, tf):
    n_lt = seq // tl
    rows = bs * tl
    n_f = D_FF // tf
    lead = F_HALO if n_lt > 1 else 0
    in_specs = [pl.BlockSpec((rows, D_MODEL), lambda b, i, f: (b * n_lt + i, 0))]
    args = [x2d]
    if n_lt > 1:
        assert bs == 1
        per = tl // F_HALO
        in_specs.append(pl.BlockSpec((F_HALO, D_MODEL),
                                     lambda b, i, f: (jnp.maximum((b * n_lt + i) * per - 1, 0), 0)))
        args.append(x2d)
    in_specs += [pl.BlockSpec((bs, F_HALO, tf), lambda b, i, f: (b, 0, f)),
                 pl.BlockSpec((1, D_MODEL), lambda b, i, f: (0, 0)),
                 pl.BlockSpec((D_MODEL, tf), lambda b, i, f: (0, f)),
                 pl.BlockSpec((D_MODEL, tf), lambda b, i, f: (0, f + n_f)),
                 pl.BlockSpec((FFN_K, tf), lambda b, i, f: (0, f)),
                 pl.BlockSpec((1, tf), lambda b, i, f: (0, f)),
                 pl.BlockSpec((tf, D_MODEL), lambda b, i, f: (f, 0))]
    args += [hist, norm_g, wup_bf, wup_bf, wdw, bdw, wdown_bf]
    y, tail = pl.pallas_call(
        functools.partial(_ffn_kernel, bs=bs, tl=tl, n_lt=n_lt, tf=tf),
        grid=(n // bs, n_lt, n_f),
        in_specs=in_specs,
        out_specs=[pl.BlockSpec((rows, D_MODEL), lambda b, i, f: (b * n_lt + i, 0)),
                   pl.BlockSpec((bs, F_HALO, tf), lambda b, i, f: (b * n_lt + i, 0, f))],
        out_shape=[jax.ShapeDtypeStruct((n * seq, D_MODEL), F32),
                   jax.ShapeDtypeStruct((n * n_lt, F_HALO, D_FF), F32)],
        scratch_shapes=[pltpu.VMEM((lead + rows, D_MODEL), BF16),
                        pltpu.VMEM((bs, F_HALO + tl, tf), F32),
                        pltpu.VMEM((rows, D_MODEL), F32)],
        compiler_params=_params(3),
        name="conv_ffn",
    )(*args)
    tail = tail.reshape(n, n_lt, F_HALO, D_FF)[:, n_lt - 1, F_HALO - (FFN_K - 1):, :]
    return y, tail


def _front_pad(a, rows):
    return jnp.pad(a, ((0, 0), (rows - a.shape[1], 0), (0, 0)))


def _trunk(x, a_cache, b_cache, f_cache, W, sample):
    n, seq, _ = x.shape
    m = n * seq
    tm = 256 if sample else 1024
    x2 = x.reshape(m, D_MODEL)
    wid = H_G * DH
    ia = ib = ic = 0
    n_a = (DEPTH + 2) // 3
    new_a = None
    new_b, new_c, new_f = [], [], []
    for layer in range(DEPTH):
        kind = layer % 3
        if kind == 0:
            gains = jnp.stack([W['a_q_gain'][ia], W['a_k_gain'][ia],
                               jnp.ones_like(W['a_q_gain'][ia])], axis=1).reshape(N_GROUPS * 3, 1, DH)
            if not sample:
                qkv, new_a = _qkv_proj(x2, W['a_norm'][ia][None], W['a_w_in'][ia], gains, 512,
                                       seq=seq, layer=ia, caches=new_a, n_layers=n_a)
                att = _attn_prompt(qkv.reshape(n, seq, N_GROUPS * 3 * wid))
                x2 = _outproj(att.reshape(m, wid), W['a_w_out'][ia], x2, tm)
            else:
                qkv, _ = _qkv_proj(x2, W['a_norm'][ia][None], W['a_w_in'][ia], gains, tm)
                qkv3 = qkv.reshape(n, seq, N_GROUPS * 3 * wid)
                parts, filled = [], []
                for g in range(N_GROUPS):
                    o_g, lse_g, c_new = _attn_sample(qkv3, a_cache[g], ia, g, None if new_a is None else new_a[g])
                    parts += [o_g.reshape(m, wid), lse_g.reshape(m, wid)]
                    filled.append(c_new)
                new_a = filled
                x2 = _merge_outproj(parts, W['a_w_out'][ia], x2)
            ia += 1
        elif kind == 1:
            u = _glu_proj(x2, W['b_norm'][ib][None], W['b_w_pw1'][ib], W['b_b_pw1'][ib][None], tm, 512)
            u3 = u.reshape(n, seq, D_MODEL)
            buf = b_cache[ib] if sample else jnp.zeros((n, CONV_K - 1, D_MODEL), F32)
            hist = _front_pad(buf, B_HALO)
            bs, tl = (n, seq) if sample else (1, 512)
            x2 = _bconv(u3, hist, W['b_w_dw'][ib], W['b_b_dw'][ib][None], W['b_ln_g'][ib][None],
                        W['b_ln_b'][ib][None], W['b_w_pw2'][ib], W['b_b_pw2'][ib][None], x2, bs, tl)
            cat = jnp.concatenate([buf, u3], axis=1)
            new_b.append(cat[:, cat.shape[1] - (CONV_K - 1):])
            ib += 1
        else:
            uv = _gelu_ln_proj(x2, W['c_norm'][ic][None], W['c_w_in'][ic], W['c_ln_g'][ic][None],
                               W['c_ln_b'][ic][None], tm)
            if sample:
                mix = jnp.tile(W['c_w_s'][ic][:, :seq, :seq], (1, n, n))
                brow = jnp.tile(W['c_b_s'][ic].T[:seq], (n, 1))
                rows, l_seq = m, seq
                new_c.append(uv[:, D_MODEL:].reshape(n, seq, D_MODEL))
            else:
                mix = W['c_w_s'][ic]
                brow = W['c_b_s'][ic].T
                rows, l_seq = 512, CHUNK
            x2 = _cgate(uv, mix, brow, W['c_w_out'][ic], x2, rows, l_seq)
            ic += 1
        buf = f_cache[layer] if sample else jnp.zeros((n, FFN_K - 1, D_FF), F32)
        hist = _front_pad(buf, F_HALO)
        bs, tl, tf = (n, seq, 256) if sample else (1, 512, D_FF // 2)
        x2, tail = _ffn(x2, hist, W['f_norm'][layer][None], W['f_w_up'][layer], W['f_w_dw'][layer],
                        W['f_b_dw'][layer][None], W['f_w_down'][layer], n, seq, bs, tl, tf)
        new_f.append(tail)
    new_a = [a.reshape(n_a, n, a.shape[2] // KV_ROWS, 2, H_G, DH) for a in new_a]
    out = [x2.reshape(n, seq, D_MODEL)] + new_a + [jnp.stack(new_b, axis=0)]
    if sample:
        out.append(jnp.stack(new_c, axis=0))
    out.append(jnp.stack(new_f, axis=0))
    return out


def kernel(x_prompt, x_sample, cache_a_g0_kv, cache_a_g1_kv, cache_a_g2_kv, state_b_conv, state_ffn_conv,
           a_norm, a_w_in, a_q_gain, a_k_gain, a_w_out,
           b_norm, b_w_pw1, b_b_pw1, b_w_dw, b_b_dw, b_ln_g, b_ln_b, b_w_pw2, b_b_pw2,
           c_norm, c_w_in, c_ln_g, c_ln_b, c_w_s, c_b_s, c_w_out,
           f_norm, f_w_up, f_w_dw, f_b_dw, f_w_down):
    W = dict(a_norm=a_norm, a_w_in=a_w_in.astype(BF16), a_q_gain=a_q_gain, a_k_gain=a_k_gain,
             a_w_out=a_w_out.astype(BF16),
             b_norm=b_norm, b_w_pw1=b_w_pw1.astype(BF16), b_b_pw1=b_b_pw1, b_w_dw=b_w_dw, b_b_dw=b_b_dw,
             b_ln_g=b_ln_g, b_ln_b=b_ln_b, b_w_pw2=b_w_pw2.astype(BF16), b_b_pw2=b_b_pw2,
             c_norm=c_norm, c_w_in=c_w_in.astype(BF16), c_ln_g=c_ln_g, c_ln_b=c_ln_b, c_w_s=c_w_s, c_b_s=c_b_s,
             c_w_out=c_w_out.astype(BF16),
             f_norm=f_norm, f_w_up=f_w_up.astype(BF16), f_w_dw=f_w_dw, f_b_dw=f_b_dw,
             f_w_down=f_w_down.astype(BF16))
    yp, pa0, pa1, pa2, pb, pf = _trunk(x_prompt, None, None, None, W, sample=False)
    caches = [c.reshape(c.shape[0], c.shape[1], c.shape[2] * KV_ROWS, DH)
              for c in (cache_a_g0_kv, cache_a_g1_kv, cache_a_g2_kv)]
    ys, sa0, sa1, sa2, sb, sc, sf = _trunk(x_sample, caches, state_b_conv, state_ffn_conv, W, sample=True)
    return (yp, ys, pa0, pa1, pa2, pb, pf, sa0, sa1, sa2, sb, sc, sf)
```

```python
import functools

import jax
import jax.numpy as jnp
from jax import lax
from jax.experimental import pallas as pl
from jax.experimental.pallas import tpu as pltpu

F32 = jnp.float32
BF16 = jnp.bfloat16

D_MODEL = 1024
DEPTH = 4
A_GROUPS = ((128, 1), (512, 4), (2048, 16))
N_GROUPS = 3
H_G = 8
DH = 128
ATT_BLK = 128
CONV_K = 31
C_GROUPS = 8
CHUNK = 128
D_FF = 2816
FFN_K = 3
EPS = 1e-6
NEG = -1e30

LANES = 128
F32_SUBLANES = 8
BF16_SUBLANES = 16
VMEM_LIMIT_BYTES = 56 * 1024 * 1024

B_HALO = 32
F_HALO = 16
FFN_PARTS = 2


def _params(n_axes):
    return pltpu.CompilerParams(dimension_semantics=("arbitrary",) * n_axes,
                                vmem_limit_bytes=VMEM_LIMIT_BYTES)


def _rms_norm(x, g):
    return x * lax.rsqrt(jnp.mean(x * x, axis=-1, keepdims=True) + EPS) * g


def _layer_norm(x, g, b):
    mu = jnp.mean(x, axis=-1, keepdims=True)
    xc = x - mu
    return xc * lax.rsqrt(jnp.mean(xc * xc, axis=-1, keepdims=True) + EPS) * g + b


def _idiv(x, n):
    assert n & (n - 1) == 0
    return jnp.right_shift(x, n.bit_length() - 1)


def _imod(x, n):
    assert n & (n - 1) == 0
    return jnp.bitwise_and(x, n - 1)


def _dot(a, b):
    return jnp.dot(a, b, preferred_element_type=F32)


def _dot_nt(a, b):
    return lax.dot_general(a, b, (((1,), (1,)), ((), ())), preferred_element_type=F32)


KV_ROWS = 2 * H_G


def _qk_norm_heads(y, gain):
    heads = []
    for h in range(H_G):
        yh = y[:, h * DH:(h + 1) * DH]
        ms = jnp.mean(yh * yh, axis=-1, keepdims=True)
        heads.append(yh * lax.rsqrt(ms + EPS) * gain)
    return heads


def _qkv_kernel(x_ref, g_ref, w_ref, gain_ref, o_ref, xn_ref):
    j = pl.program_id(1)

    @pl.when(j == 0)
    def _():
        xn_ref[...] = _rms_norm(x_ref[...], g_ref[...]).astype(BF16)

    y = _dot(xn_ref[...], w_ref[...])

    @pl.when(j % 3 < 2)
    def _():
        heads = _qk_norm_heads(y, gain_ref[0])
        for h in range(H_G):
            o_ref[:, h * DH:(h + 1) * DH] = heads[h]

    @pl.when(j % 3 == 2)
    def _():
        o_ref[...] = y


def _qkv_group_kernel(*refs, tm, dil, n_t, keep, aliased):
    x_ref, g_ref, w_ref, gain_ref = refs[:4]
    og_ref, c_ref, st_ref = refs[4 + aliased:]
    wid = H_G * DH
    xn = _rms_norm(x_ref[...], g_ref[...]).astype(BF16)
    n_rows = min(keep, tm)
    in_keep = pl.program_id(0) % n_t >= n_t - max(keep // tm, 1)
    for sec in range(3):
        y = _dot(xn, w_ref[:, sec * wid:(sec + 1) * wid])
        if sec < 2:
            heads = _qk_norm_heads(y, gain_ref[sec])
        else:
            heads = [y[:, h * DH:(h + 1) * DH] for h in range(H_G)]
        if dil == 1:
            for h in range(H_G):
                og_ref[0, :, sec * wid + h * DH:sec * wid + (h + 1) * DH] = heads[h].astype(BF16)
        else:
            for h in range(H_G):
                st_ref[sec, h] = heads[h]
            for r in range(dil):
                for h in range(H_G):
                    og_ref[r, :, sec * wid + h * DH:sec * wid + (h + 1) * DH] = (
                        st_ref[sec, h, pl.ds(r, tm // dil, stride=dil), :].astype(BF16))
        if sec > 0:
            def emit(kv=sec - 1, heads=heads):
                by_head = jnp.stack([hd[tm - n_rows:tm, :] for hd in heads], axis=0)
                c_ref[:, kv, :, :] = jnp.swapaxes(by_head, 0, 1)

            if keep >= n_t * tm:
                emit()
            else:
                pl.when(in_keep)(emit)


def _qkv_proj(x2d, norm_g, w_bf, gains, tm):
    m = x2d.shape[0]
    n_blk = N_GROUPS * 3
    wid = H_G * DH
    return pl.pallas_call(
        _qkv_kernel,
        grid=(m // tm, n_blk),
        in_specs=[pl.BlockSpec((tm, D_MODEL), lambda i, j: (i, 0)),
                  pl.BlockSpec((1, D_MODEL), lambda i, j: (0, 0)),
                  pl.BlockSpec((D_MODEL, wid), lambda i, j: (0, j)),
                  pl.BlockSpec((1, 1, DH), lambda i, j: (j, 0, 0))],
        out_specs=pl.BlockSpec((tm, wid), lambda i, j: (i, j)),
        out_shape=jax.ShapeDtypeStruct((m, n_blk * wid), F32),
        scratch_shapes=[pltpu.VMEM((tm, D_MODEL), BF16)],
        compiler_params=_params(2),
        name="qkv_proj",
    )(x2d, norm_g, w_bf, gains)


def _qkv_group_proj(x2d, norm_g, w_bf, gains, g, tm, seq, layer, cache, n_layers):
    m = x2d.shape[0]
    wid = H_G * DH
    win, dil = A_GROUPS[g]
    keep = min(win, seq)
    n_t = seq // tm
    n_rows = min(keep, tm)
    tiles = max(keep // tm, 1)
    in_specs = [pl.BlockSpec((tm, D_MODEL), lambda i: (i, 0)),
                pl.BlockSpec((1, D_MODEL), lambda i: (0, 0)),
                pl.BlockSpec((D_MODEL, 3 * wid), lambda i: (0, g)),
                pl.BlockSpec((2, 1, DH), lambda i: (g, 0, 0))]
    args = [x2d, norm_g, w_bf, gains]
    aliases = {}
    if cache is not None:
        in_specs.append(pl.BlockSpec(memory_space=pl.ANY))
        args.append(cache)
        aliases[4] = 1
    return pl.pallas_call(
        functools.partial(_qkv_group_kernel, tm=tm, dil=dil, n_t=n_t, keep=keep, aliased=len(aliases)),
        grid=(m // tm,),
        in_specs=in_specs,
        out_specs=[pl.BlockSpec((None, dil, tm // dil, 3 * wid), lambda i: (i // n_t, 0, i % n_t, 0)),
                   pl.BlockSpec((None, None, n_rows, 2, H_G, DH),
                                lambda i: (layer, i // n_t, jnp.maximum(i % n_t - (n_t - tiles), 0), 0, 0, 0))],
        out_shape=[jax.ShapeDtypeStruct((m // seq, dil, seq // dil, 3 * wid), BF16),
                   jax.ShapeDtypeStruct((n_layers, m // seq, keep, 2, H_G, DH), F32)],
        input_output_aliases=aliases,
        scratch_shapes=[pltpu.VMEM((3, H_G, tm, DH), F32)],
        compiler_params=_params(1),
        name="qkv_group_d%d" % dil,
    )(*args)


def _attn_band_kernel(q_ref, k_ref, v_ref, o_ref, lse_ref, *, n_res, n_sub, linked):
    scale = DH ** -0.5
    blk = ATT_BLK
    n_blk = n_sub // blk
    n_keys = 2 * blk
    a_i = lax.broadcasted_iota(jnp.int32, (blk, n_keys), 0)
    c_i = lax.broadcasted_iota(jnp.int32, (blk, n_keys), 1)
    bias_first = jnp.where((c_i < blk) & (c_i <= a_i), 0.0, NEG)
    own_later = (c_i >= blk) & ((c_i - blk) <= a_i)
    if linked:
        own_later = own_later | ((c_i < blk) & (c_i >= a_i))
    bias_later = jnp.where(own_later, 0.0, NEG)
    lane = lax.broadcasted_iota(jnp.int32, (blk, LANES), 1)

    def body(t, carry):
        res = t // n_blk
        i = t % n_blk
        rq = pl.ds(pl.multiple_of(i * blk, blk), blk)
        rk = pl.ds(pl.multiple_of(jnp.maximum(i - 1, 0) * blk, blk), n_keys)
        bias = jnp.where(i == 0, bias_first, bias_later)
        lse_t = jnp.zeros((blk, LANES), F32)
        for h in range(H_G):
            cols = slice(h * DH, (h + 1) * DH)
            s = _dot_nt(q_ref[res, rq, cols], k_ref[res, rk, cols]) * scale + bias
            m = jnp.max(s, axis=-1, keepdims=True)
            p = jnp.exp(s - m)
            l = jnp.sum(p, axis=-1, keepdims=True)
            o_ref[res, rq, cols] = (_dot(p.astype(BF16), v_ref[res, rk, cols]) / l).astype(o_ref.dtype)
            lse_t = jnp.where(lane == h, m + jnp.log(l), lse_t)
        lse_ref[res, rq, :] = lse_t
        return carry

    lax.fori_loop(0, n_res * n_blk, body, 0)


def _attn_band(qkv_g):
    nb, dil, n_sub, _ = qkv_g.shape
    wid = H_G * DH
    linked = n_sub > ATT_BLK
    n_rng, n_row = dil, n_sub
    if not linked:
        n_rng, n_row = dil // 2, 2 * n_sub
        qkv_g = qkv_g.reshape(nb, n_rng, n_row, 3 * wid)
    n_res = min(n_rng, 4)
    o, lse = pl.pallas_call(
        functools.partial(_attn_band_kernel, n_res=n_res, n_sub=n_row, linked=linked),
        grid=(nb, n_rng // n_res),
        in_specs=[pl.BlockSpec((None, n_res, n_row, wid), lambda b, r, c=c: (b, r, 0, c)) for c in range(3)],
        out_specs=[pl.BlockSpec((None, n_res, n_row, wid), lambda b, r: (b, r, 0, 0)),
                   pl.BlockSpec((None, n_res, n_row, LANES), lambda b, r: (b, r, 0, 0))],
        out_shape=[jax.ShapeDtypeStruct((nb, n_rng, n_row, wid), BF16),
                   jax.ShapeDtypeStruct((nb, n_rng, n_row, LANES), F32)],
        compiler_params=_params(2),
        name="attn_band_d%d" % dil,
    )(qkv_g, qkv_g, qkv_g)
    return o.reshape(nb, dil, n_sub, wid), lse.reshape(nb, dil, n_sub, LANES)


def _attn_sample_kernel(*refs, dil, band, n_buf, tc, n_c, t_new, n_alias):
    q_ref, kn_ref, vn_ref, c_ref = refs[:4]
    nxt_ref = refs[4] if n_c > 1 else None
    n_in = (5 if n_c > 1 else 4) + n_alias
    o_ref, lse_ref, cout_ref, m_ref, l_ref, acc_ref = refs[n_in:]
    c = pl.program_id(1)
    scale = DH ** -0.5
    wid = H_G * DH
    n_q = H_G * t_new

    q = q_ref[0]
    qt = jnp.concatenate([q] * H_G, axis=0)
    row_h = _idiv(lax.broadcasted_iota(jnp.int32, (n_q, wid), 0), t_new)
    col_h = _idiv(lax.broadcasted_iota(jnp.int32, (n_q, wid), 1), DH)
    qbd = jnp.where(row_h == col_h, qt, 0.0).astype(BF16)

    @pl.when(c == 0)
    def _():
        m_ref[...] = jnp.full(m_ref.shape, NEG, F32)
        l_ref[...] = jnp.zeros(l_ref.shape, F32)
        acc_ref[...] = jnp.zeros(acc_ref.shape, F32)

    def update(keys, vals, idx):
        t = _imod(lax.broadcasted_iota(jnp.int32, idx.shape, 0), t_new)
        back = n_buf + t - idx
        valid = (back >= 0) & (back <= dil * band) & (_imod(back, dil) == 0)
        s = jnp.where(valid, _dot_nt(qbd, keys) * scale, NEG)
        m_prev = m_ref[...]
        m_new = jnp.maximum(m_prev, jnp.max(s, axis=-1, keepdims=True))
        alpha = jnp.exp(m_prev - m_new)
        p = jnp.where(valid, jnp.exp(s - m_new), 0.0)
        l_ref[...] = alpha * l_ref[...] + jnp.sum(p, axis=-1, keepdims=True)
        acc_ref[...] = alpha * acc_ref[...] + _dot(p.astype(BF16), vals)
        m_ref[...] = m_new

    def all_heads(kv):
        return jnp.concatenate([c_ref[0, pl.ds(kv * H_G + h, tc, stride=KV_ROWS), :] for h in range(H_G)],
                               axis=1).astype(BF16)

    idx_c = c * tc + lax.broadcasted_iota(jnp.int32, (n_q, tc), 1)
    update(all_heads(0), all_heads(1), idx_c)

    body = (tc - t_new) * KV_ROWS
    cout_ref[0, 0:body, :] = c_ref[0, t_new * KV_ROWS:tc * KV_ROWS, :]
    last = c == n_c - 1

    if n_c > 1:
        @pl.when(jnp.logical_not(last))
        def _():
            cout_ref[0, body:tc * KV_ROWS, :] = nxt_ref[0]

    @pl.when(last)
    def _():
        kn = kn_ref[0]
        vn = vn_ref[0]
        for h in range(H_G):
            cout_ref[0, pl.ds(body + h, t_new, stride=KV_ROWS), :] = kn[:, h * DH:(h + 1) * DH]
            cout_ref[0, pl.ds(body + H_G + h, t_new, stride=KV_ROWS), :] = vn[:, h * DH:(h + 1) * DH]
        pad = jnp.zeros((LANES - t_new, wid), F32)
        idx_n = n_buf + lax.broadcasted_iota(jnp.int32, (n_q, LANES), 1)
        update(jnp.concatenate([kn, pad], axis=0).astype(BF16),
               jnp.concatenate([vn, pad], axis=0).astype(BF16), idx_n)
        l = l_ref[...]
        o_full = acc_ref[...] / l
        lse = m_ref[...] + jnp.log(l)
        for h in range(H_G):
            o_ref[0, :, h * DH:(h + 1) * DH] = o_full[h * t_new:(h + 1) * t_new, h * DH:(h + 1) * DH]
            lse_ref[0, :, h * DH:(h + 1) * DH] = jnp.broadcast_to(lse[h * t_new:(h + 1) * t_new], (t_new, DH))


def _attn_sample(qkv3, cache4, layer, g, new_cache):
    ndb, t_new, _ = qkv3.shape
    n_layers = cache4.shape[0]
    n_buf = cache4.shape[2] // KV_ROWS
    win, dil = A_GROUPS[g]
    band = win // dil
    wid = H_G * DH
    tc = min(n_buf, 512)
    n_c = n_buf // tc
    in_specs = [pl.BlockSpec((1, t_new, wid), lambda b, c: (b, 0, g * 3)),
                pl.BlockSpec((1, t_new, wid), lambda b, c: (b, 0, g * 3 + 1)),
                pl.BlockSpec((1, t_new, wid), lambda b, c: (b, 0, g * 3 + 2)),
                pl.BlockSpec((None, 1, tc * KV_ROWS, DH), lambda b, c: (layer, b, c, 0))]
    args = [qkv3, qkv3, qkv3, cache4]
    if n_c > 1:
        per = tc // t_new
        last_blk = n_buf // t_new - 1
        in_specs.append(pl.BlockSpec((None, 1, t_new * KV_ROWS, DH),
                                     lambda b, c: (layer, b, jnp.minimum((c + 1) * per, last_blk), 0)))
        args.append(cache4)
    aliases = {}
    if new_cache is not None:
        aliases[len(args)] = 2
        in_specs.append(pl.BlockSpec(memory_space=pl.ANY))
        args.append(new_cache)
    n_q = H_G * t_new
    return pl.pallas_call(
        functools.partial(_attn_sample_kernel, dil=dil, band=band, n_buf=n_buf, tc=tc, n_c=n_c, t_new=t_new,
                          n_alias=len(aliases)),
        grid=(ndb, n_c),
        in_specs=in_specs,
        out_specs=[pl.BlockSpec((1, t_new, wid), lambda b, c: (b, 0, 0)),
                   pl.BlockSpec((1, t_new, wid), lambda b, c: (b, 0, 0)),
                   pl.BlockSpec((None, 1, tc * KV_ROWS, DH), lambda b, c: (layer, b, c, 0))],
        out_shape=[jax.ShapeDtypeStruct((ndb, t_new, wid), F32),
                   jax.ShapeDtypeStruct((ndb, t_new, wid), F32),
                   jax.ShapeDtypeStruct((n_layers, ndb, n_buf * KV_ROWS, DH), F32)],
        input_output_aliases=aliases,
        scratch_shapes=[pltpu.VMEM((n_q, 1), F32), pltpu.VMEM((n_q, 1), F32), pltpu.VMEM((n_q, wid), F32)],
        compiler_params=_params(2),
        name="attn_sample_g%d" % g,
    )(*args)


def _band_outproj_kernel(o0, o1, o2, l0, l1, l2, w_ref, res_ref, out_ref, ln_ref, att_ref, *, tm):
    o_refs, l_refs = (o0, o1, o2), (l0, l1, l2)
    for g in range(N_GROUPS):
        dil = A_GROUPS[g][1]
        for r in range(dil):
            rows = pl.ds(r, tm // dil, stride=dil) if dil > 1 else pl.ds(0, tm)
            ln_ref[g, rows, :] = l_refs[g][r]
    la, lb, lc = ln_ref[0], ln_ref[1], ln_ref[2]
    mx = jnp.maximum(jnp.maximum(la, lb), lc)
    es = [jnp.exp(la - mx), jnp.exp(lb - mx), jnp.exp(lc - mx)]
    den = es[0] + es[1] + es[2]
    ws = [e / den for e in es]

    t_i = lax.broadcasted_iota(jnp.int32, (tm, tm), 0)
    c_i = lax.broadcasted_iota(jnp.int32, (tm, tm), 1)
    outs = []
    for g in range(N_GROUPS):
        dil = A_GROUPS[g][1]
        if dil == 1:
            outs.append(o_refs[g][0].astype(F32))
        else:
            src = _imod(t_i, dil) * (tm // dil) + _idiv(t_i, dil)
            sel = jnp.where(c_i == src, 1.0, 0.0).astype(BF16)
            outs.append(_dot(sel, o_refs[g][...].reshape(tm, H_G * DH)))
    for h in range(H_G):
        cols = slice(h * DH, (h + 1) * DH)
        att = ws[0][:, h:h + 1] * outs[0][:, cols]
        for g in range(1, N_GROUPS):
            att = att + ws[g][:, h:h + 1] * outs[g][:, cols]
        att_ref[:, cols] = att.astype(BF16)
    out_ref[...] = res_ref[...] + _dot(att_ref[...], w_ref[...])


def _band_outproj(outs, lses, w_bf, res2d, seq, tm):
    m = res2d.shape[0]
    n_t = seq // tm
    wid = H_G * DH
    in_specs = []
    for width in (wid, LANES):
        for _, dil in A_GROUPS:
            in_specs.append(pl.BlockSpec((None, dil, tm // dil, width), lambda i: (i // n_t, 0, i % n_t, 0)))
    in_specs += [pl.BlockSpec((D_MODEL, D_MODEL), lambda i: (0, 0)),
                 pl.BlockSpec((tm, D_MODEL), lambda i: (i, 0))]
    return pl.pallas_call(
        functools.partial(_band_outproj_kernel, tm=tm),
        grid=(m // tm,),
        in_specs=in_specs,
        out_specs=pl.BlockSpec((tm, D_MODEL), lambda i: (i, 0)),
        out_shape=jax.ShapeDtypeStruct((m, D_MODEL), F32),
        scratch_shapes=[pltpu.VMEM((N_GROUPS, tm, LANES), F32), pltpu.VMEM((tm, wid), BF16)],
        compiler_params=_params(1),
        name="a_outproj",
    )(*outs, *lses, w_bf, res2d)


def _merge_outproj_kernel(o0, l0, o1, l1, o2, l2, w_ref, res_ref, o_ref):
    la, lb, lc = l0[...], l1[...], l2[...]
    mx = jnp.maximum(jnp.maximum(la, lb), lc)
    ea, eb, ec = jnp.exp(la - mx), jnp.exp(lb - mx), jnp.exp(lc - mx)
    att = (ea * o0[...] + eb * o1[...] + ec * o2[...]) / (ea + eb + ec)
    o_ref[...] = res_ref[...] + _dot(att.astype(BF16), w_ref[...])


def _merge_outproj(parts, w_bf, res2d):
    m = res2d.shape[0]
    row = pl.BlockSpec((m, D_MODEL), lambda i: (0, 0))
    return pl.pallas_call(
        _merge_outproj_kernel,
        grid=(1,),
        in_specs=[row] * 6 + [pl.BlockSpec((D_MODEL, D_MODEL), lambda i: (0, 0)), row],
        out_specs=row,
        out_shape=jax.ShapeDtypeStruct((m, D_MODEL), F32),
        compiler_params=_params(1),
        name="a_merge_outproj",
    )(*parts, w_bf, res2d)


def _glu_kernel(x_ref, g_ref, wa_ref, wb_ref, ba_ref, bb_ref, o_ref, xn_ref):
    @pl.when(pl.program_id(1) == 0)
    def _():
        xn_ref[...] = _rms_norm(x_ref[...], g_ref[...]).astype(BF16)

    xn = xn_ref[...]
    a = _dot(xn, wa_ref[...]) + ba_ref[...]
    b = _dot(xn, wb_ref[...]) + bb_ref[...]
    o_ref[...] = a * jax.nn.sigmoid(b)


def _glu_proj(x2d, norm_g, w_bf, bias, tm, tn):
    m = x2d.shape[0]
    nb = D_MODEL // tn
    return pl.pallas_call(
        _glu_kernel,
        grid=(m // tm, nb),
        in_specs=[pl.BlockSpec((tm, D_MODEL), lambda i, j: (i, 0)),
                  pl.BlockSpec((1, D_MODEL), lambda i, j: (0, 0)),
                  pl.BlockSpec((D_MODEL, tn), lambda i, j: (0, j)),
                  pl.BlockSpec((D_MODEL, tn), lambda i, j: (0, j + nb)),
                  pl.BlockSpec((1, tn), lambda i, j: (0, j)),
                  pl.BlockSpec((1, tn), lambda i, j: (0, j + nb))],
        out_specs=pl.BlockSpec((tm, tn), lambda i, j: (i, j)),
        out_shape=jax.ShapeDtypeStruct((m, D_MODEL), F32),
        scratch_shapes=[pltpu.VMEM((tm, D_MODEL), BF16)],
        compiler_params=_params(2),
        name="b_glu_proj",
    )(x2d, norm_g, w_bf, w_bf, bias, bias)


def _bconv_kernel(*refs, bs, tl, n_lt):
    if n_lt > 1:
        (u_ref, halo_ref, hist_ref, wdw_ref, bdw_ref, lng_ref, lnb_ref, w2_ref, b2_ref, res_ref,
         o_ref, cat_ref, y_ref) = refs
    else:
        (u_ref, hist_ref, wdw_ref, bdw_ref, lng_ref, lnb_ref, w2_ref, b2_ref, res_ref,
         o_ref, cat_ref, y_ref) = refs
    if n_lt > 1:
        first = pl.program_id(1) == 0

        @pl.when(first)
        def _():
            cat_ref[:, 0:B_HALO, :] = hist_ref[...]

        @pl.when(jnp.logical_not(first))
        def _():
            cat_ref[:, 0:B_HALO, :] = halo_ref[...]
    else:
        cat_ref[:, 0:B_HALO, :] = hist_ref[...]
    cat_ref[:, B_HALO:B_HALO + tl, :] = u_ref[...]

    rc = min(tl, 32)
    lc = LANES
    n_rc = tl // rc
    lead = B_HALO - (CONV_K - 1)
    sub = F32_SUBLANES

    def body(it, carry):
        b = it // n_rc
        r0 = pl.multiple_of((it % n_rc) * rc, sub)
        pieces = []
        for c0 in range(0, D_MODEL, lc):
            win = cat_ref[b, pl.ds(r0, rc + B_HALO), c0:c0 + lc]
            shifted = [win] + [win[s:s + rc + B_HALO - sub, :] for s in range(1, sub)]
            acc = jnp.broadcast_to(bdw_ref[:, c0:c0 + lc], (rc, lc))
            for k in range(CONV_K):
                q, s = divmod(lead + k, sub)
                acc = acc + shifted[s][q * sub:q * sub + rc, :] * wdw_ref[k:k + 1, c0:c0 + lc]
            pieces.append(acc)
        conv = jnp.concatenate(pieces, axis=1)
        y = _layer_norm(conv, lng_ref[...], lnb_ref[...])
        y_ref[pl.ds(pl.multiple_of(b * tl + r0, F32_SUBLANES), rc), :] = y * jax.nn.sigmoid(y)
        return carry

    lax.fori_loop(0, bs * n_rc, body, 0)
    o_ref[...] = res_ref[...] + _dot(y_ref[...].astype(BF16), w2_ref[...]) + b2_ref[...]


def _bconv(u3, hist, wdw, bdw, lng, lnb, w2_bf, b2, res2d, bs, tl):
    n, seq, _ = u3.shape
    n_lt = seq // tl
    rows = bs * tl
    vec = pl.BlockSpec((1, D_MODEL), lambda b, i: (0, 0))
    in_specs = [pl.BlockSpec((bs, tl, D_MODEL), lambda b, i: (b, i, 0))]
    args = [u3]
    if n_lt > 1:
        per = tl // B_HALO
        in_specs.append(pl.BlockSpec((bs, B_HALO, D_MODEL), lambda b, i: (b, jnp.maximum(i * per - 1, 0), 0)))
        args.append(u3)
    in_specs += [pl.BlockSpec((bs, B_HALO, D_MODEL), lambda b, i: (b, 0, 0)),
                 pl.BlockSpec((CONV_K, D_MODEL), lambda b, i: (0, 0)),
                 vec, vec, vec,
                 pl.BlockSpec((D_MODEL, D_MODEL), lambda b, i: (0, 0)),
                 vec,
                 pl.BlockSpec((rows, D_MODEL), lambda b, i: (b * n_lt + i, 0))]
    args += [hist, wdw, bdw, lng, lnb, w2_bf, b2, res2d]
    return pl.pallas_call(
        functools.partial(_bconv_kernel, bs=bs, tl=tl, n_lt=n_lt),
        grid=(n // bs, n_lt),
        in_specs=in_specs,
        out_specs=pl.BlockSpec((rows, D_MODEL), lambda b, i: (b * n_lt + i, 0)),
        out_shape=jax.ShapeDtypeStruct((n * seq, D_MODEL), F32),
        scratch_shapes=[pltpu.VMEM((bs, B_HALO + tl, D_MODEL), F32), pltpu.VMEM((rows, D_MODEL), F32)],
        compiler_params=_params(2),
        name="b_conv_proj",
    )(*args)


def _gelu_ln_kernel(x_ref, g_ref, w_ref, lng_ref, lnb_ref, o_ref, xn_ref):
    j = pl.program_id(1)

    @pl.when(j == 0)
    def _():
        xn_ref[...] = _rms_norm(x_ref[...], g_ref[...]).astype(BF16)

    y = _dot(xn_ref[...], w_ref[...])
    y = 0.5 * y * (1.0 + lax.erf(y * (0.5 ** 0.5)))

    @pl.when(j == 0)
    def _():
        o_ref[...] = y

    @pl.when(j == 1)
    def _():
        o_ref[...] = _layer_norm(y, lng_ref[...], lnb_ref[...])


def _gelu_ln_proj(x2d, norm_g, w_bf, lng, lnb, tm):
    m = x2d.shape[0]
    vec = pl.BlockSpec((1, D_MODEL), lambda i, j: (0, 0))
    return pl.pallas_call(
        _gelu_ln_kernel,
        grid=(m // tm, 2),
        in_specs=[pl.BlockSpec((tm, D_MODEL), lambda i, j: (i, 0)), vec,
                  pl.BlockSpec((D_MODEL, D_MODEL), lambda i, j: (0, j)), vec, vec],
        out_specs=pl.BlockSpec((tm, D_MODEL), lambda i, j: (i, j)),
        out_shape=jax.ShapeDtypeStruct((m, 2 * D_MODEL), F32),
        scratch_shapes=[pltpu.VMEM((tm, D_MODEL), BF16)],
        compiler_params=_params(2),
        name="c_in_proj",
    )(x2d, norm_g, w_bf, lng, lnb)


def _cgate_kernel(u_ref, v_ref, mix_ref, brow_ref, w_ref, res_ref, o_ref, h_ref, *, r_mix, l_seq, rows):
    gw = D_MODEL // C_GROUPS
    ri = lax.broadcasted_iota(jnp.int32, (r_mix, r_mix), 0)
    ci = lax.broadcasted_iota(jnp.int32, (r_mix, r_mix), 1)
    valid = (_idiv(ri, l_seq) == _idiv(ci, l_seq)) & (_imod(ci, l_seq) <= _imod(ri, l_seq))
    for g in range(C_GROUPS):
        mg = jnp.where(valid, mix_ref[g], 0.0).astype(BF16)
        bcol = brow_ref[:, g:g + 1]
        cols = slice(g * gw, (g + 1) * gw)
        for mb in range(rows // r_mix):
            rws = slice(mb * r_mix, (mb + 1) * r_mix)
            s = _dot(mg, v_ref[rws, cols].astype(BF16)) + bcol
            h_ref[rws, cols] = (u_ref[rws, cols] * s).astype(BF16)
    o_ref[...] = res_ref[...] + _dot(h_ref[...], w_ref[...])


def _cgate(uv2d, mix, brow, w_bf, res2d, rows, l_seq):
    m = res2d.shape[0]
    r_mix = mix.shape[1]
    return pl.pallas_call(
        functools.partial(_cgate_kernel, r_mix=r_mix, l_seq=l_seq, rows=rows),
        grid=(m // rows,),
        in_specs=[pl.BlockSpec((rows, D_MODEL), lambda i: (i, 0)),
                  pl.BlockSpec((rows, D_MODEL), lambda i: (i, 1)),
                  pl.BlockSpec((C_GROUPS, r_mix, r_mix), lambda i: (0, 0, 0)),
                  pl.BlockSpec((r_mix, C_GROUPS), lambda i: (0, 0)),
                  pl.BlockSpec((D_MODEL, D_MODEL), lambda i: (0, 0)),
                  pl.BlockSpec((rows, D_MODEL), lambda i: (i, 0))],
        out_specs=pl.BlockSpec((rows, D_MODEL), lambda i: (i, 0)),
        out_shape=jax.ShapeDtypeStruct((m, D_MODEL), F32),
        scratch_shapes=[pltpu.VMEM((rows, D_MODEL), BF16)],
        compiler_params=_params(1),
        name="c_gate_proj",
    )(uv2d, uv2d, mix, brow, w_bf, res2d)


def _ffn_kernel(*refs, bs, tl, n_lt, tf):
    if n_lt > 1:
        (x_ref, xh_ref, hist_ref, g_ref, wg_ref, wu_ref, wdw_ref, bdw_ref, wd_ref,
         o_ref, tail_ref, xn_ref, acc_ref) = refs
    else:
        (x_ref, hist_ref, g_ref, wg_ref, wu_ref, wdw_ref, bdw_ref, wd_ref,
         o_ref, tail_ref, xn_ref, hs_ref, acc_ref) = refs
    i = pl.program_id(1)
    f = pl.program_id(2)
    n_f = pl.num_programs(2)
    rows = bs * tl
    lead = F_HALO if n_lt > 1 else 0

    @pl.when(f == 0)
    def _():
        x = x_ref[...]
        if n_lt > 1:
            xn_ref[0:F_HALO, :] = _rms_norm(xh_ref[...], g_ref[...]).astype(BF16)
        xn_ref[lead:lead + rows, :] = _rms_norm(x, g_ref[...]).astype(BF16)
        acc_ref[...] = x

    base = F_HALO - (FFN_K - 1)
    if n_lt > 1:
        pr = tl // FFN_PARTS
        for p in range(FFN_PARTS):
            gate = _dot(xn_ref[p * pr:p * pr + pr + F_HALO, :], wg_ref[...])
            up = _dot(xn_ref[F_HALO + p * pr:F_HALO + (p + 1) * pr, :], wu_ref[...])
            prev = gate[0:F_HALO, :]
            if p == 0:
                prev = jnp.where(i == 0, hist_ref[0], prev)
            hs = jnp.concatenate([prev, gate[F_HALO:, :]], axis=0)
            gc = bdw_ref[...]
            for k in range(FFN_K):
                gc = gc + hs[base + k:base + k + pr, :] * wdw_ref[k:k + 1, :]
            act = (gc * jax.nn.sigmoid(gc)) * up
            acc_ref[p * pr:(p + 1) * pr, :] += _dot(act.astype(BF16), wd_ref[...])
            if p == FFN_PARTS - 1:
                tail_ref[0] = hs[pr:pr + F_HALO, :]
    else:
        gate = _dot(xn_ref[...], wg_ref[...])
        up = _dot(xn_ref[...], wu_ref[...])
        hs_ref[:, 0:F_HALO, :] = hist_ref[...]
        hs_ref[:, F_HALO:F_HALO + tl, :] = gate.reshape(bs, tl, tf)
        hs = hs_ref[...]
        gc = bdw_ref[...]
        for k in range(FFN_K):
            gc = gc + hs[:, base + k:base + k + tl, :] * wdw_ref[k:k + 1, :]
        act = (gc * jax.nn.sigmoid(gc)) * up.reshape(bs, tl, tf)
        acc_ref[...] += _dot(act.reshape(rows, tf).astype(BF16), wd_ref[...])
        tail_ref[...] = hs[:, tl:tl + F_HALO, :]

    @pl.when(f == n_f - 1)
    def _():
        o_ref[...] = acc_ref[...]


def _ffn(x2d, hist, norm_g, wup_bf, wdw, bdw, wdown_bf, n, seq, bs, tl, tf):
    n_lt = seq // tl
    rows = bs * tl
    n_f = D_FF // tf
    lead = F_HALO if n_lt > 1 else 0
    in_specs = [pl.BlockSpec((rows, D_MODEL), lambda b, i, f: (b * n_lt + i, 0))]
    args = [x2d]
    if n_lt > 1:
        assert bs == 1
        per = tl // F_HALO
        in_specs.append(pl.BlockSpec((F_HALO, D_MODEL),
                                     lambda b, i, f: (jnp.maximum((b * n_lt + i) * per - 1, 0), 0)))
        args.append(x2d)
    in_specs += [pl.BlockSpec((bs, F_HALO, tf), lambda b, i, f: (b, 0, f)),
                 pl.BlockSpec((1, D_MODEL), lambda b, i, f: (0, 0)),
                 pl.BlockSpec((D_MODEL, tf), lambda b, i, f: (0, f)),
                 pl.BlockSpec((D_MODEL, tf), lambda b, i, f: (0, f + n_f)),
                 pl.BlockSpec((FFN_K, tf), lambda b, i, f: (0, f)),
                 pl.BlockSpec((1, tf), lambda b, i, f: (0, f)),
                 pl.BlockSpec((tf, D_MODEL), lambda b, i, f: (f, 0))]
    args += [hist, norm_g, wup_bf, wup_bf, wdw, bdw, wdown_bf]
    y, tail = pl.pallas_call(
        functools.partial(_ffn_kernel, bs=bs, tl=tl, n_lt=n_lt, tf=tf),
        grid=(n // bs, n_lt, n_f),
        in_specs=in_specs,
        out_specs=[pl.BlockSpec((rows, D_MODEL), lambda b, i, f: (b * n_lt + i, 0)),
                   pl.BlockSpec((bs, F_HALO, tf), lambda b, i, f: (b * n_lt + i, 0, f))],
        out_shape=[jax.ShapeDtypeStruct((n * seq, D_MODEL), F32),
                   jax.ShapeDtypeStruct((n * n_lt, F_HALO, D_FF), F32)],
        scratch_shapes=([pltpu.VMEM((lead + rows, D_MODEL), BF16)]
                        + ([] if n_lt > 1 else [pltpu.VMEM((bs, F_HALO + tl, tf), F32)])
                        + [pltpu.VMEM((rows, D_MODEL), F32)]),
        compiler_params=_params(3),
        name="conv_ffn",
    )(*args)
    tail = tail.reshape(n, n_lt, F_HALO, D_FF)[:, n_lt - 1, F_HALO - (FFN_K - 1):, :]
    return y, tail


def _front_pad(a, rows):
    return jnp.pad(a, ((0, 0), (rows - a.shape[1], 0), (0, 0)))


def _trunk(x, a_cache, b_cache, f_cache, W, sample):
    n, seq, _ = x.shape
    m = n * seq
    tm = 256 if sample else 1024
    x2 = x.reshape(m, D_MODEL)
    wid = H_G * DH
    ia = ib = ic = 0
    n_a = (DEPTH + 2) // 3
    new_a = None
    new_b, new_c, new_f = [], [], []
    for layer in range(DEPTH):
        kind = layer % 3
        if kind == 0:
            gains = jnp.stack([W['a_q_gain'][ia], W['a_k_gain'][ia],
                               jnp.ones_like(W['a_q_gain'][ia])], axis=1).reshape(N_GROUPS * 3, 1, DH)
            if not sample:
                gains_qk = jnp.stack([W['a_q_gain'][ia], W['a_k_gain'][ia]], axis=1).reshape(N_GROUPS * 2, 1, DH)
                banded, filled = [], []
                for g in range(N_GROUPS):
                    qkv_g, c_new = _qkv_group_proj(x2, W['a_norm'][ia][None], W['a_w_in'][ia], gains_qk, g, 512, seq,
                                                   ia, None if new_a is None else new_a[g], n_a)
                    banded.append(_attn_band(qkv_g))
                    filled.append(c_new)
                new_a = filled
                x2 = _band_outproj([o for o, _ in banded], [l for _, l in banded], W['a_w_out'][ia], x2, seq, 512)
            else:
                qkv = _qkv_proj(x2, W['a_norm'][ia][None], W['a_w_in'][ia], gains, tm)
                qkv3 = qkv.reshape(n, seq, N_GROUPS * 3 * wid)
                parts, filled = [], []
                for g in range(N_GROUPS):
                    o_g, lse_g, c_new = _attn_sample(qkv3, a_cache[g], ia, g, None if new_a is None else new_a[g])
                    parts += [o_g.reshape(m, wid), lse_g.reshape(m, wid)]
                    filled.append(c_new)
                new_a = filled
                x2 = _merge_outproj(parts, W['a_w_out'][ia], x2)
            ia += 1
        elif kind == 1:
            u = _glu_proj(x2, W['b_norm'][ib][None], W['b_w_pw1'][ib], W['b_b_pw1'][ib][None], tm, 512)
            u3 = u.reshape(n, seq, D_MODEL)
            buf = b_cache[ib] if sample else jnp.zeros((n, CONV_K - 1, D_MODEL), F32)
            hist = _front_pad(buf, B_HALO)
            bs, tl = (n, seq) if sample else (1, 512)
            x2 = _bconv(u3, hist, W['b_w_dw'][ib], W['b_b_dw'][ib][None], W['b_ln_g'][ib][None],
                        W['b_ln_b'][ib][None], W['b_w_pw2'][ib], W['b_b_pw2'][ib][None], x2, bs, tl)
            cat = jnp.concatenate([buf, u3], axis=1)
            new_b.append(cat[:, cat.shape[1] - (CONV_K - 1):])
            ib += 1
        else:
            uv = _gelu_ln_proj(x2, W['c_norm'][ic][None], W['c_w_in'][ic], W['c_ln_g'][ic][None],
                               W['c_ln_b'][ic][None], tm)
            if sample:
                mix = jnp.tile(W['c_w_s'][ic][:, :seq, :seq], (1, n, n))
                brow = jnp.tile(W['c_b_s'][ic].T[:seq], (n, 1))
                rows, l_seq = m, seq
                new_c.append(uv[:, D_MODEL:].reshape(n, seq, D_MODEL))
            else:
                mix = W['c_w_s'][ic]
                brow = W['c_b_s'][ic].T
                rows, l_seq = 512, CHUNK
            x2 = _cgate(uv, mix, brow, W['c_w_out'][ic], x2, rows, l_seq)
            ic += 1
        buf = f_cache[layer] if sample else jnp.zeros((n, FFN_K - 1, D_FF), F32)
        hist = _front_pad(buf, F_HALO)
        bs, tl, tf = (n, seq, 256) if sample else (1, 512, D_FF // 2)
        x2, tail = _ffn(x2, hist, W['f_norm'][layer][None], W['f_w_up'][layer], W['f_w_dw'][layer],
                        W['f_b_dw'][layer][None], W['f_w_down'][layer], n, seq, bs, tl, tf)
        new_f.append(tail)
    if sample:
        new_a = [a.reshape(n_a, n, a.shape[2] // KV_ROWS, 2, H_G, DH) for a in new_a]
    out = [x2.reshape(n, seq, D_MODEL)] + new_a + [jnp.stack(new_b, axis=0)]
    if sample:
        out.append(jnp.stack(new_c, axis=0))
    out.append(jnp.stack(new_f, axis=0))
    return out


def kernel(x_prompt, x_sample, cache_a_g0_kv, cache_a_g1_kv, cache_a_g2_kv, state_b_conv, state_ffn_conv,
           a_norm, a_w_in, a_q_gain, a_k_gain, a_w_out,
           b_norm, b_w_pw1, b_b_pw1, b_w_dw, b_b_dw, b_ln_g, b_ln_b, b_w_pw2, b_b_pw2,
           c_norm, c_w_in, c_ln_g, c_ln_b, c_w_s, c_b_s, c_w_out,
           f_norm, f_w_up, f_w_dw, f_b_dw, f_w_down):
    W = dict(a_norm=a_norm, a_w_in=a_w_in.astype(BF16), a_q_gain=a_q_gain, a_k_gain=a_k_gain,
             a_w_out=a_w_out.astype(BF16),
             b_norm=b_norm, b_w_pw1=b_w_pw1.astype(BF16), b_b_pw1=b_b_pw1, b_w_dw=b_w_dw, b_b_dw=b_b_dw,
             b_ln_g=b_ln_g, b_ln_b=b_ln_b, b_w_pw2=b_w_pw2.astype(BF16), b_b_pw2=b_b_pw2,
             c_norm=c_norm, c_w_in=c_w_in.astype(BF16), c_ln_g=c_ln_g, c_ln_b=c_ln_b, c_w_s=c_w_s, c_b_s=c_b_s,
             c_w_out=c_w_out.astype(BF16),
             f_norm=f_norm, f_w_up=f_w_up.astype(BF16), f_w_dw=f_w_dw, f_b_dw=f_b_dw,
             f_w_down=f_w_down.astype(BF16))
    yp, pa0, pa1, pa2, pb, pf = _trunk(x_prompt, None, None, None, W, sample=False)
    caches = [c.reshape(c.shape[0], c.shape[1], c.shape[2] * KV_ROWS, DH)
              for c in (cache_a_g0_kv, cache_a_g1_kv, cache_a_g2_kv)]
    ys, sa0, sa1, sa2, sb, sc, sf = _trunk(x_sample, caches, state_b_conv, state_ffn_conv, W, sample=True)
    return (yp, ys, pa0, pa1, pa2, pb, pf, sa0, sa1, sa2, sb, sc, sf)
```

```python
import functools

import jax
import jax.numpy as jnp
from jax import lax
from jax.experimental import pallas as pl
from jax.experimental.pallas import tpu as pltpu

F32 = jnp.float32
BF16 = jnp.bfloat16

D_MODEL = 1024
DEPTH = 4
A_GROUPS = ((128, 1), (512, 4), (2048, 16))
N_GROUPS = 3
H_G = 8
DH = 128
ATT_BLK = 128
CONV_K = 31
C_GROUPS = 8
CHUNK = 128
D_FF = 2816
FFN_K = 3
EPS = 1e-6
NEG = -1e30

LANES = 128
F32_SUBLANES = 8
BF16_SUBLANES = 16
VMEM_LIMIT_BYTES = 56 * 1024 * 1024

B_HALO = 32
F_HALO = 16
FFN_PART_ROWS = 256


def _params(n_axes):
    return pltpu.CompilerParams(dimension_semantics=("arbitrary",) * n_axes,
                                vmem_limit_bytes=VMEM_LIMIT_BYTES)


def _rms_norm(x, g):
    return x * lax.rsqrt(jnp.mean(x * x, axis=-1, keepdims=True) + EPS) * g


def _layer_norm(x, g, b):
    mu = jnp.mean(x, axis=-1, keepdims=True)
    xc = x - mu
    return xc * lax.rsqrt(jnp.mean(xc * xc, axis=-1, keepdims=True) + EPS) * g + b


def _idiv(x, n):
    assert n & (n - 1) == 0
    return jnp.right_shift(x, n.bit_length() - 1)


def _imod(x, n):
    assert n & (n - 1) == 0
    return jnp.bitwise_and(x, n - 1)


def _dot(a, b):
    return jnp.dot(a, b, preferred_element_type=F32)


def _dot_nt(a, b):
    return lax.dot_general(a, b, (((1,), (1,)), ((), ())), preferred_element_type=F32)


KV_ROWS = 2 * H_G


def _qk_norm_heads(y, gain):
    heads = []
    for h in range(H_G):
        yh = y[:, h * DH:(h + 1) * DH]
        ms = jnp.mean(yh * yh, axis=-1, keepdims=True)
        heads.append(yh * lax.rsqrt(ms + EPS) * gain)
    return heads


def _qkv_kernel(x_ref, g_ref, w_ref, gain_ref, o_ref, xn_ref):
    j = pl.program_id(1)

    @pl.when(j == 0)
    def _():
        xn_ref[...] = _rms_norm(x_ref[...], g_ref[...]).astype(BF16)

    y = _dot(xn_ref[...], w_ref[...])

    @pl.when(j % 3 < 2)
    def _():
        heads = _qk_norm_heads(y, gain_ref[0])
        for h in range(H_G):
            o_ref[:, h * DH:(h + 1) * DH] = heads[h]

    @pl.when(j % 3 == 2)
    def _():
        o_ref[...] = y


def _qkv_group_kernel(*refs, tm, dil, n_t, keep, aliased):
    x_ref, g_ref, w_ref, gain_ref = refs[:4]
    og_ref, c_ref, st_ref = refs[4 + aliased:]
    wid = H_G * DH
    xn = _rms_norm(x_ref[...], g_ref[...]).astype(BF16)
    n_rows = min(keep, tm)
    in_keep = pl.program_id(0) % n_t >= n_t - max(keep // tm, 1)
    for sec in range(3):
        y = _dot(xn, w_ref[:, sec * wid:(sec + 1) * wid])
        if sec < 2:
            heads = _qk_norm_heads(y, gain_ref[sec])
        else:
            heads = [y[:, h * DH:(h + 1) * DH] for h in range(H_G)]
        for h in range(H_G):
            st_ref[sec, h] = heads[h]
        for r in range(dil):
            rows = pl.ds(r, tm // dil, stride=dil) if dil > 1 else pl.ds(0, tm)
            for h in range(H_G):
                og_ref[r, :, sec * wid + h * DH:sec * wid + (h + 1) * DH] = st_ref[sec, h, rows, :].astype(BF16)
        if sec > 0:
            def emit(kv=sec - 1, sec=sec):
                by_head = st_ref[sec, :, tm - n_rows:tm, :]
                c_ref[:, kv, :, :] = jnp.swapaxes(by_head, 0, 1)

            if keep >= n_t * tm:
                emit()
            else:
                pl.when(in_keep)(emit)


def _qkv_proj(x2d, norm_g, w_bf, gains, tm):
    m = x2d.shape[0]
    n_blk = N_GROUPS * 3
    wid = H_G * DH
    return pl.pallas_call(
        _qkv_kernel,
        grid=(m // tm, n_blk),
        in_specs=[pl.BlockSpec((tm, D_MODEL), lambda i, j: (i, 0)),
                  pl.BlockSpec((1, D_MODEL), lambda i, j: (0, 0)),
                  pl.BlockSpec((D_MODEL, wid), lambda i, j: (0, j)),
                  pl.BlockSpec((1, 1, DH), lambda i, j: (j, 0, 0))],
        out_specs=pl.BlockSpec((tm, wid), lambda i, j: (i, j)),
        out_shape=jax.ShapeDtypeStruct((m, n_blk * wid), F32),
        scratch_shapes=[pltpu.VMEM((tm, D_MODEL), BF16)],
        compiler_params=_params(2),
        name="qkv_proj",
    )(x2d, norm_g, w_bf, gains)


def _qkv_group_proj(x2d, norm_g, w_bf, gains, g, tm, seq, layer, cache, n_layers):
    m = x2d.shape[0]
    wid = H_G * DH
    win, dil = A_GROUPS[g]
    keep = min(win, seq)
    n_t = seq // tm
    n_rows = min(keep, tm)
    tiles = max(keep // tm, 1)
    in_specs = [pl.BlockSpec((tm, D_MODEL), lambda i: (i, 0)),
                pl.BlockSpec((1, D_MODEL), lambda i: (0, 0)),
                pl.BlockSpec((D_MODEL, 3 * wid), lambda i: (0, g)),
                pl.BlockSpec((2, 1, DH), lambda i: (g, 0, 0))]
    args = [x2d, norm_g, w_bf, gains]
    aliases = {}
    if cache is not None:
        in_specs.append(pl.BlockSpec(memory_space=pl.ANY))
        args.append(cache)
        aliases[4] = 1
    return pl.pallas_call(
        functools.partial(_qkv_group_kernel, tm=tm, dil=dil, n_t=n_t, keep=keep, aliased=len(aliases)),
        grid=(m // tm,),
        in_specs=in_specs,
        out_specs=[pl.BlockSpec((None, dil, tm // dil, 3 * wid), lambda i: (i // n_t, 0, i % n_t, 0)),
                   pl.BlockSpec((None, None, n_rows, 2, H_G, DH),
                                lambda i: (layer, i // n_t, jnp.maximum(i % n_t - (n_t - tiles), 0), 0, 0, 0))],
        out_shape=[jax.ShapeDtypeStruct((m // seq, dil, seq // dil, 3 * wid), BF16),
                   jax.ShapeDtypeStruct((n_layers, m // seq, keep, 2, H_G, DH), F32)],
        input_output_aliases=aliases,
        scratch_shapes=[pltpu.VMEM((3, H_G, tm, DH), F32)],
        compiler_params=_params(1),
        name="qkv_group_d%d" % dil,
    )(*args)


def _attn_band_kernel(q_ref, k_ref, v_ref, o_ref, lse_ref, *, n_res, n_sub, linked):
    scale = DH ** -0.5
    blk = ATT_BLK
    n_blk = n_sub // blk
    n_keys = 2 * blk
    a_i = lax.broadcasted_iota(jnp.int32, (blk, n_keys), 0)
    c_i = lax.broadcasted_iota(jnp.int32, (blk, n_keys), 1)
    bias_first = jnp.where((c_i < blk) & (c_i <= a_i), 0.0, NEG)
    own_later = (c_i >= blk) & ((c_i - blk) <= a_i)
    if linked:
        own_later = own_later | ((c_i < blk) & (c_i >= a_i))
    bias_later = jnp.where(own_later, 0.0, NEG)
    lane = lax.broadcasted_iota(jnp.int32, (blk, LANES), 1)

    def body(t, carry):
        res = t // n_blk
        i = t % n_blk
        rq = pl.ds(pl.multiple_of(i * blk, blk), blk)
        rk = pl.ds(pl.multiple_of(jnp.maximum(i - 1, 0) * blk, blk), n_keys)
        bias = jnp.where(i == 0, bias_first, bias_later)
        lse_t = jnp.zeros((blk, LANES), F32)
        for h in range(H_G):
            cols = slice(h * DH, (h + 1) * DH)
            s = _dot_nt(q_ref[res, rq, cols], k_ref[res, rk, cols]) * scale + bias
            m = jnp.max(s, axis=-1, keepdims=True)
            p = jnp.exp(s - m)
            l = jnp.sum(p, axis=-1, keepdims=True)
            o_ref[res, rq, cols] = (_dot(p.astype(BF16), v_ref[res, rk, cols]) / l).astype(o_ref.dtype)
            lse_t = jnp.where(lane == h, m + jnp.log(l), lse_t)
        lse_ref[res, rq, :] = lse_t
        return carry

    lax.fori_loop(0, n_res * n_blk, body, 0)


def _attn_band(qkv_g):
    nb, dil, n_sub, _ = qkv_g.shape
    wid = H_G * DH
    linked = n_sub > ATT_BLK
    n_rng, n_row = dil, n_sub
    if not linked:
        n_rng, n_row = dil // 2, 2 * n_sub
        qkv_g = qkv_g.reshape(nb, n_rng, n_row, 3 * wid)
    n_res = min(n_rng, 4)
    o, lse = pl.pallas_call(
        functools.partial(_attn_band_kernel, n_res=n_res, n_sub=n_row, linked=linked),
        grid=(nb, n_rng // n_res),
        in_specs=[pl.BlockSpec((None, n_res, n_row, wid), lambda b, r, c=c: (b, r, 0, c)) for c in range(3)],
        out_specs=[pl.BlockSpec((None, n_res, n_row, wid), lambda b, r: (b, r, 0, 0)),
                   pl.BlockSpec((None, n_res, n_row, LANES), lambda b, r: (b, r, 0, 0))],
        out_shape=[jax.ShapeDtypeStruct((nb, n_rng, n_row, wid), BF16),
                   jax.ShapeDtypeStruct((nb, n_rng, n_row, LANES), F32)],
        compiler_params=_params(2),
        name="attn_band_d%d" % dil,
    )(qkv_g, qkv_g, qkv_g)
    return o.reshape(nb, dil, n_sub, wid), lse.reshape(nb, dil, n_sub, LANES)


def _attn_sample_kernel(*refs, dil, band, n_buf, tc, n_c, t_new, n_alias):
    q_ref, kn_ref, vn_ref, c_ref = refs[:4]
    nxt_ref = refs[4] if n_c > 1 else None
    n_in = (5 if n_c > 1 else 4) + n_alias
    o_ref, lse_ref, cout_ref, m_ref, l_ref, acc_ref = refs[n_in:]
    c = pl.program_id(1)
    scale = DH ** -0.5
    wid = H_G * DH
    n_q = H_G * t_new

    q = q_ref[0]
    qt = jnp.concatenate([q] * H_G, axis=0)
    row_h = _idiv(lax.broadcasted_iota(jnp.int32, (n_q, wid), 0), t_new)
    col_h = _idiv(lax.broadcasted_iota(jnp.int32, (n_q, wid), 1), DH)
    qbd = jnp.where(row_h == col_h, qt, 0.0).astype(BF16)

    @pl.when(c == 0)
    def _():
        m_ref[...] = jnp.full(m_ref.shape, NEG, F32)
        l_ref[...] = jnp.zeros(l_ref.shape, F32)
        acc_ref[...] = jnp.zeros(acc_ref.shape, F32)

    def update(keys, vals, idx):
        t = _imod(lax.broadcasted_iota(jnp.int32, idx.shape, 0), t_new)
        back = n_buf + t - idx
        valid = (back >= 0) & (back <= dil * band) & (_imod(back, dil) == 0)
        s = jnp.where(valid, _dot_nt(qbd, keys) * scale, NEG)
        m_prev = m_ref[...]
        m_new = jnp.maximum(m_prev, jnp.max(s, axis=-1, keepdims=True))
        alpha = jnp.exp(m_prev - m_new)
        p = jnp.where(valid, jnp.exp(s - m_new), 0.0)
        l_ref[...] = alpha * l_ref[...] + jnp.sum(p, axis=-1, keepdims=True)
        acc_ref[...] = alpha * acc_ref[...] + _dot(p.astype(BF16), vals)
        m_ref[...] = m_new

    def all_heads(kv):
        return jnp.concatenate([c_ref[0, pl.ds(kv * H_G + h, tc, stride=KV_ROWS), :] for h in range(H_G)],
                               axis=1).astype(BF16)

    idx_c = c * tc + lax.broadcasted_iota(jnp.int32, (n_q, tc), 1)
    update(all_heads(0), all_heads(1), idx_c)

    body = (tc - t_new) * KV_ROWS
    cout_ref[0, 0:body, :] = c_ref[0, t_new * KV_ROWS:tc * KV_ROWS, :]
    last = c == n_c - 1

    if n_c > 1:
        @pl.when(jnp.logical_not(last))
        def _():
            cout_ref[0, body:tc * KV_ROWS, :] = nxt_ref[0]

    @pl.when(last)
    def _():
        kn = kn_ref[0]
        vn = vn_ref[0]
        for h in range(H_G):
            cout_ref[0, pl.ds(body + h, t_new, stride=KV_ROWS), :] = kn[:, h * DH:(h + 1) * DH]
            cout_ref[0, pl.ds(body + H_G + h, t_new, stride=KV_ROWS), :] = vn[:, h * DH:(h + 1) * DH]
        pad = jnp.zeros((LANES - t_new, wid), F32)
        idx_n = n_buf + lax.broadcasted_iota(jnp.int32, (n_q, LANES), 1)
        update(jnp.concatenate([kn, pad], axis=0).astype(BF16),
               jnp.concatenate([vn, pad], axis=0).astype(BF16), idx_n)
        l = l_ref[...]
        o_full = acc_ref[...] / l
        lse = m_ref[...] + jnp.log(l)
        for h in range(H_G):
            o_ref[0, :, h * DH:(h + 1) * DH] = o_full[h * t_new:(h + 1) * t_new, h * DH:(h + 1) * DH]
            lse_ref[0, :, h * DH:(h + 1) * DH] = jnp.broadcast_to(lse[h * t_new:(h + 1) * t_new], (t_new, DH))


def _attn_sample(qkv3, cache4, layer, g, new_cache):
    ndb, t_new, _ = qkv3.shape
    n_layers = cache4.shape[0]
    n_buf = cache4.shape[2] // KV_ROWS
    win, dil = A_GROUPS[g]
    band = win // dil
    wid = H_G * DH
    tc = min(n_buf, 512)
    n_c = n_buf // tc
    in_specs = [pl.BlockSpec((1, t_new, wid), lambda b, c: (b, 0, g * 3)),
                pl.BlockSpec((1, t_new, wid), lambda b, c: (b, 0, g * 3 + 1)),
                pl.BlockSpec((1, t_new, wid), lambda b, c: (b, 0, g * 3 + 2)),
                pl.BlockSpec((None, 1, tc * KV_ROWS, DH), lambda b, c: (layer, b, c, 0))]
    args = [qkv3, qkv3, qkv3, cache4]
    if n_c > 1:
        per = tc // t_new
        last_blk = n_buf // t_new - 1
        in_specs.append(pl.BlockSpec((None, 1, t_new * KV_ROWS, DH),
                                     lambda b, c: (layer, b, jnp.minimum((c + 1) * per, last_blk), 0)))
        args.append(cache4)
    aliases = {}
    if new_cache is not None:
        aliases[len(args)] = 2
        in_specs.append(pl.BlockSpec(memory_space=pl.ANY))
        args.append(new_cache)
    n_q = H_G * t_new
    return pl.pallas_call(
        functools.partial(_attn_sample_kernel, dil=dil, band=band, n_buf=n_buf, tc=tc, n_c=n_c, t_new=t_new,
                          n_alias=len(aliases)),
        grid=(ndb, n_c),
        in_specs=in_specs,
        out_specs=[pl.BlockSpec((1, t_new, wid), lambda b, c: (b, 0, 0)),
                   pl.BlockSpec((1, t_new, wid), lambda b, c: (b, 0, 0)),
                   pl.BlockSpec((None, 1, tc * KV_ROWS, DH), lambda b, c: (layer, b, c, 0))],
        out_shape=[jax.ShapeDtypeStruct((ndb, t_new, wid), F32),
                   jax.ShapeDtypeStruct((ndb, t_new, wid), F32),
                   jax.ShapeDtypeStruct((n_layers, ndb, n_buf * KV_ROWS, DH), F32)],
        input_output_aliases=aliases,
        scratch_shapes=[pltpu.VMEM((n_q, 1), F32), pltpu.VMEM((n_q, 1), F32), pltpu.VMEM((n_q, wid), F32)],
        compiler_params=_params(2),
        name="attn_sample_g%d" % g,
    )(*args)


def _band_outproj_kernel(o0, o1, o2, l0, l1, l2, w_ref, res_ref, out_ref, ln_ref, att_ref, *, tm):
    o_refs, l_refs = (o0, o1, o2), (l0, l1, l2)
    for g in range(N_GROUPS):
        dil = A_GROUPS[g][1]
        for r in range(dil):
            rows = pl.ds(r, tm // dil, stride=dil) if dil > 1 else pl.ds(0, tm)
            ln_ref[g, rows, :] = l_refs[g][r]
    la, lb, lc = ln_ref[0], ln_ref[1], ln_ref[2]
    mx = jnp.maximum(jnp.maximum(la, lb), lc)
    es = [jnp.exp(la - mx), jnp.exp(lb - mx), jnp.exp(lc - mx)]
    den = es[0] + es[1] + es[2]
    ws = [e / den for e in es]

    t_i = lax.broadcasted_iota(jnp.int32, (tm, tm), 0)
    c_i = lax.broadcasted_iota(jnp.int32, (tm, tm), 1)
    outs = []
    for g in range(N_GROUPS):
        dil = A_GROUPS[g][1]
        if dil == 1:
            outs.append(o_refs[g][0].astype(F32))
        else:
            src = _imod(t_i, dil) * (tm // dil) + _idiv(t_i, dil)
            sel = jnp.where(c_i == src, 1.0, 0.0).astype(BF16)
            outs.append(_dot(sel, o_refs[g][...].reshape(tm, H_G * DH)))
    for h in range(H_G):
        cols = slice(h * DH, (h + 1) * DH)
        att = ws[0][:, h:h + 1] * outs[0][:, cols]
        for g in range(1, N_GROUPS):
            att = att + ws[g][:, h:h + 1] * outs[g][:, cols]
        att_ref[:, cols] = att.astype(BF16)
    out_ref[...] = res_ref[...] + _dot(att_ref[...], w_ref[...])


def _band_outproj(outs, lses, w_bf, res2d, seq, tm):
    m = res2d.shape[0]
    n_t = seq // tm
    wid = H_G * DH
    in_specs = []
    for width in (wid, LANES):
        for _, dil in A_GROUPS:
            in_specs.append(pl.BlockSpec((None, dil, tm // dil, width), lambda i: (i // n_t, 0, i % n_t, 0)))
    in_specs += [pl.BlockSpec((D_MODEL, D_MODEL), lambda i: (0, 0)),
                 pl.BlockSpec((tm, D_MODEL), lambda i: (i, 0))]
    return pl.pallas_call(
        functools.partial(_band_outproj_kernel, tm=tm),
        grid=(m // tm,),
        in_specs=in_specs,
        out_specs=pl.BlockSpec((tm, D_MODEL), lambda i: (i, 0)),
        out_shape=jax.ShapeDtypeStruct((m, D_MODEL), F32),
        scratch_shapes=[pltpu.VMEM((N_GROUPS, tm, LANES), F32), pltpu.VMEM((tm, wid), BF16)],
        compiler_params=_params(1),
        name="a_outproj",
    )(*outs, *lses, w_bf, res2d)


def _merge_outproj_kernel(o0, l0, o1, l1, o2, l2, w_ref, res_ref, o_ref):
    la, lb, lc = l0[...], l1[...], l2[...]
    mx = jnp.maximum(jnp.maximum(la, lb), lc)
    ea, eb, ec = jnp.exp(la - mx), jnp.exp(lb - mx), jnp.exp(lc - mx)
    att = (ea * o0[...] + eb * o1[...] + ec * o2[...]) / (ea + eb + ec)
    o_ref[...] = res_ref[...] + _dot(att.astype(BF16), w_ref[...])


def _merge_outproj(parts, w_bf, res2d):
    m = res2d.shape[0]
    row = pl.BlockSpec((m, D_MODEL), lambda i: (0, 0))
    return pl.pallas_call(
        _merge_outproj_kernel,
        grid=(1,),
        in_specs=[row] * 6 + [pl.BlockSpec((D_MODEL, D_MODEL), lambda i: (0, 0)), row],
        out_specs=row,
        out_shape=jax.ShapeDtypeStruct((m, D_MODEL), F32),
        compiler_params=_params(1),
        name="a_merge_outproj",
    )(*parts, w_bf, res2d)


def _glu_kernel(x_ref, g_ref, w_ref, b_ref, o_ref, *, tn):
    xn = _rms_norm(x_ref[...], g_ref[...]).astype(BF16)
    for c in range(0, D_MODEL, tn):
        a = _dot(xn, w_ref[:, c:c + tn]) + b_ref[:, c:c + tn]
        b = _dot(xn, w_ref[:, D_MODEL + c:D_MODEL + c + tn]) + b_ref[:, D_MODEL + c:D_MODEL + c + tn]
        o_ref[:, c:c + tn] = a * jax.nn.sigmoid(b)


def _glu_proj(x2d, norm_g, w_bf, bias, tm, tn):
    m = x2d.shape[0]
    return pl.pallas_call(
        functools.partial(_glu_kernel, tn=tn),
        grid=(m // tm,),
        in_specs=[pl.BlockSpec((tm, D_MODEL), lambda i: (i, 0)),
                  pl.BlockSpec((1, D_MODEL), lambda i: (0, 0)),
                  pl.BlockSpec((D_MODEL, 2 * D_MODEL), lambda i: (0, 0)),
                  pl.BlockSpec((1, 2 * D_MODEL), lambda i: (0, 0))],
        out_specs=pl.BlockSpec((tm, D_MODEL), lambda i: (i, 0)),
        out_shape=jax.ShapeDtypeStruct((m, D_MODEL), F32),
        compiler_params=_params(1),
        name="b_glu_proj",
    )(x2d, norm_g, w_bf, bias)


def _bconv_kernel(*refs, bs, tl, n_lt):
    if n_lt > 1:
        (u_ref, halo_ref, hist_ref, wdw_ref, bdw_ref, lng_ref, lnb_ref, w2_ref, b2_ref, res_ref,
         o_ref, cat_ref, sh_ref, y_ref) = refs
    else:
        (u_ref, hist_ref, wdw_ref, bdw_ref, lng_ref, lnb_ref, w2_ref, b2_ref, res_ref,
         o_ref, cat_ref, sh_ref, y_ref) = refs
    if n_lt > 1:
        first = pl.program_id(1) == 0

        @pl.when(first)
        def _():
            cat_ref[:, 0:B_HALO, :] = hist_ref[...]

        @pl.when(jnp.logical_not(first))
        def _():
            cat_ref[:, 0:B_HALO, :] = halo_ref[...]
    else:
        cat_ref[:, 0:B_HALO, :] = hist_ref[...]
    cat_ref[:, B_HALO:B_HALO + tl, :] = u_ref[...]

    rc = min(tl, 32)
    lc = LANES
    n_rc = tl // rc
    lead = B_HALO - (CONV_K - 1)
    sub = F32_SUBLANES

    for s in range(1, sub):
        sh_ref[s - 1] = cat_ref[:, s:s + tl + B_HALO - sub, :]

    def body(it, carry):
        b = it // n_rc
        r0 = pl.multiple_of((it % n_rc) * rc, sub)
        pieces = []
        for c0 in range(0, D_MODEL, lc):
            acc = jnp.broadcast_to(bdw_ref[:, c0:c0 + lc], (rc, lc))
            for k in range(CONV_K):
                q, s = divmod(lead + k, sub)
                rows = pl.ds(pl.multiple_of(r0 + q * sub, sub), rc)
                src = cat_ref[b, rows, c0:c0 + lc] if s == 0 else sh_ref[s - 1, b, rows, c0:c0 + lc]
                acc = acc + src * wdw_ref[k:k + 1, c0:c0 + lc]
            pieces.append(acc)
        conv = jnp.concatenate(pieces, axis=1)
        y = _layer_norm(conv, lng_ref[...], lnb_ref[...])
        y_ref[pl.ds(pl.multiple_of(b * tl + r0, F32_SUBLANES), rc), :] = y * jax.nn.sigmoid(y)
        return carry

    lax.fori_loop(0, bs * n_rc, body, 0)
    o_ref[...] = res_ref[...] + _dot(y_ref[...].astype(BF16), w2_ref[...]) + b2_ref[...]


def _bconv(u3, hist, wdw, bdw, lng, lnb, w2_bf, b2, res2d, bs, tl):
    n, seq, _ = u3.shape
    n_lt = seq // tl
    rows = bs * tl
    vec = pl.BlockSpec((1, D_MODEL), lambda b, i: (0, 0))
    in_specs = [pl.BlockSpec((bs, tl, D_MODEL), lambda b, i: (b, i, 0))]
    args = [u3]
    if n_lt > 1:
        per = tl // B_HALO
        in_specs.append(pl.BlockSpec((bs, B_HALO, D_MODEL), lambda b, i: (b, jnp.maximum(i * per - 1, 0), 0)))
        args.append(u3)
    in_specs += [pl.BlockSpec((bs, B_HALO, D_MODEL), lambda b, i: (b, 0, 0)),
                 pl.BlockSpec((CONV_K, D_MODEL), lambda b, i: (0, 0)),
                 vec, vec, vec,
                 pl.BlockSpec((D_MODEL, D_MODEL), lambda b, i: (0, 0)),
                 vec,
                 pl.BlockSpec((rows, D_MODEL), lambda b, i: (b * n_lt + i, 0))]
    args += [hist, wdw, bdw, lng, lnb, w2_bf, b2, res2d]
    return pl.pallas_call(
        functools.partial(_bconv_kernel, bs=bs, tl=tl, n_lt=n_lt),
        grid=(n // bs, n_lt),
        in_specs=in_specs,
        out_specs=pl.BlockSpec((rows, D_MODEL), lambda b, i: (b * n_lt + i, 0)),
        out_shape=jax.ShapeDtypeStruct((n * seq, D_MODEL), F32),
        scratch_shapes=[pltpu.VMEM((bs, B_HALO + tl, D_MODEL), F32),
                        pltpu.VMEM((F32_SUBLANES - 1, bs, B_HALO + tl - F32_SUBLANES, D_MODEL), F32),
                        pltpu.VMEM((rows, D_MODEL), F32)],
        compiler_params=_params(2),
        name="b_conv_proj",
    )(*args)


def _gelu(y):
    return 0.5 * y * (1.0 + lax.erf(y * (0.5 ** 0.5)))


def _gelu_ln_kernel(x_ref, g_ref, w_ref, lng_ref, lnb_ref, o_ref):
    xn = _rms_norm(x_ref[...], g_ref[...]).astype(BF16)
    o_ref[:, 0:D_MODEL] = _gelu(_dot(xn, w_ref[:, 0:D_MODEL]))
    o_ref[:, D_MODEL:2 * D_MODEL] = _layer_norm(_gelu(_dot(xn, w_ref[:, D_MODEL:2 * D_MODEL])),
                                                lng_ref[...], lnb_ref[...])


def _gelu_ln_proj(x2d, norm_g, w_bf, lng, lnb, tm):
    m = x2d.shape[0]
    vec = pl.BlockSpec((1, D_MODEL), lambda i: (0, 0))
    return pl.pallas_call(
        _gelu_ln_kernel,
        grid=(m // tm,),
        in_specs=[pl.BlockSpec((tm, D_MODEL), lambda i: (i, 0)), vec,
                  pl.BlockSpec((D_MODEL, 2 * D_MODEL), lambda i: (0, 0)), vec, vec],
        out_specs=pl.BlockSpec((tm, 2 * D_MODEL), lambda i: (i, 0)),
        out_shape=jax.ShapeDtypeStruct((m, 2 * D_MODEL), F32),
        compiler_params=_params(1),
        name="c_in_proj",
    )(x2d, norm_g, w_bf, lng, lnb)


def _cgate_kernel(u_ref, v_ref, mix_ref, brow_ref, w_ref, res_ref, o_ref, h_ref, *, r_mix, l_seq, rows):
    gw = D_MODEL // C_GROUPS
    ri = lax.broadcasted_iota(jnp.int32, (r_mix, r_mix), 0)
    ci = lax.broadcasted_iota(jnp.int32, (r_mix, r_mix), 1)
    valid = (_idiv(ri, l_seq) == _idiv(ci, l_seq)) & (_imod(ci, l_seq) <= _imod(ri, l_seq))
    for g in range(C_GROUPS):
        mg = jnp.where(valid, mix_ref[g], 0.0).astype(BF16)
        bcol = brow_ref[:, g:g + 1]
        cols = slice(g * gw, (g + 1) * gw)
        for mb in range(rows // r_mix):
            rws = slice(mb * r_mix, (mb + 1) * r_mix)
            s = _dot(mg, v_ref[rws, cols].astype(BF16)) + bcol
            h_ref[rws, cols] = (u_ref[rws, cols] * s).astype(BF16)
    o_ref[...] = res_ref[...] + _dot(h_ref[...], w_ref[...])


def _cgate(uv2d, mix, brow, w_bf, res2d, rows, l_seq):
    m = res2d.shape[0]
    r_mix = mix.shape[1]
    return pl.pallas_call(
        functools.partial(_cgate_kernel, r_mix=r_mix, l_seq=l_seq, rows=rows),
        grid=(m // rows,),
        in_specs=[pl.BlockSpec((rows, D_MODEL), lambda i: (i, 0)),
                  pl.BlockSpec((rows, D_MODEL), lambda i: (i, 1)),
                  pl.BlockSpec((C_GROUPS, r_mix, r_mix), lambda i: (0, 0, 0)),
                  pl.BlockSpec((r_mix, C_GROUPS), lambda i: (0, 0)),
                  pl.BlockSpec((D_MODEL, D_MODEL), lambda i: (0, 0)),
                  pl.BlockSpec((rows, D_MODEL), lambda i: (i, 0))],
        out_specs=pl.BlockSpec((rows, D_MODEL), lambda i: (i, 0)),
        out_shape=jax.ShapeDtypeStruct((m, D_MODEL), F32),
        scratch_shapes=[pltpu.VMEM((rows, D_MODEL), BF16)],
        compiler_params=_params(1),
        name="c_gate_proj",
    )(uv2d, uv2d, mix, brow, w_bf, res2d)


def _ffn_kernel(*refs, bs, tl, n_lt, tf):
    if n_lt > 1:
        (x_ref, xh_ref, hist_ref, g_ref, wg_ref, wu_ref, wdw_ref, bdw_ref, wd_ref,
         o_ref, tail_ref, xn_ref, acc_ref) = refs
    else:
        (x_ref, hist_ref, g_ref, wg_ref, wu_ref, wdw_ref, bdw_ref, wd_ref,
         o_ref, tail_ref, xn_ref, hs_ref, acc_ref) = refs
    i = pl.program_id(1)
    f = pl.program_id(2)
    n_f = pl.num_programs(2)
    rows = bs * tl
    lead = F_HALO if n_lt > 1 else 0

    @pl.when(f == 0)
    def _():
        x = x_ref[...]
        if n_lt > 1:
            xn_ref[0:F_HALO, :] = _rms_norm(xh_ref[...], g_ref[...]).astype(BF16)
        xn_ref[lead:lead + rows, :] = _rms_norm(x, g_ref[...]).astype(BF16)
        acc_ref[...] = x

    base = F_HALO - (FFN_K - 1)
    if n_lt > 1:
        pr = FFN_PART_ROWS
        n_parts = tl // pr
        for p in range(n_parts):
            gate = _dot(xn_ref[p * pr:p * pr + pr + F_HALO, :], wg_ref[...])
            up = _dot(xn_ref[F_HALO + p * pr:F_HALO + (p + 1) * pr, :], wu_ref[...])
            prev = gate[0:F_HALO, :]
            if p == 0:
                prev = jnp.where(i == 0, hist_ref[0], prev)
            hs = jnp.concatenate([prev, gate[F_HALO:, :]], axis=0)
            gc = bdw_ref[...]
            for k in range(FFN_K):
                gc = gc + hs[base + k:base + k + pr, :] * wdw_ref[k:k + 1, :]
            act = (gc * jax.nn.sigmoid(gc)) * up
            acc_ref[p * pr:(p + 1) * pr, :] += _dot(act.astype(BF16), wd_ref[...])
            if p == n_parts - 1:
                tail_ref[0] = hs[pr:pr + F_HALO, :]
    else:
        gate = _dot(xn_ref[...], wg_ref[...])
        up = _dot(xn_ref[...], wu_ref[...])
        hs_ref[:, 0:F_HALO, :] = hist_ref[...]
        hs_ref[:, F_HALO:F_HALO + tl, :] = gate.reshape(bs, tl, tf)
        hs = hs_ref[...]
        gc = bdw_ref[...]
        for k in range(FFN_K):
            gc = gc + hs[:, base + k:base + k + tl, :] * wdw_ref[k:k + 1, :]
        act = (gc * jax.nn.sigmoid(gc)) * up.reshape(bs, tl, tf)
        acc_ref[...] += _dot(act.reshape(rows, tf).astype(BF16), wd_ref[...])
        tail_ref[...] = hs[:, tl:tl + F_HALO, :]

    @pl.when(f == n_f - 1)
    def _():
        o_ref[...] = acc_ref[...]


def _ffn(x2d, hist, norm_g, wup_bf, wdw, bdw, wdown_bf, n, seq, bs, tl, tf):
    n_lt = seq // tl
    rows = bs * tl
    n_f = D_FF // tf
    lead = F_HALO if n_lt > 1 else 0
    in_specs = [pl.BlockSpec((rows, D_MODEL), lambda b, i, f: (b * n_lt + i, 0))]
    args = [x2d]
    if n_lt > 1:
        assert bs == 1
        per = tl // F_HALO
        in_specs.append(pl.BlockSpec((F_HALO, D_MODEL),
                                     lambda b, i, f: (jnp.maximum((b * n_lt + i) * per - 1, 0), 0)))
        args.append(x2d)
    in_specs += [pl.BlockSpec((bs, F_HALO, tf), lambda b, i, f: (b, 0, f)),
                 pl.BlockSpec((1, D_MODEL), lambda b, i, f: (0, 0)),
                 pl.BlockSpec((D_MODEL, tf), lambda b, i, f: (0, f)),
                 pl.BlockSpec((D_MODEL, tf), lambda b, i, f: (0, f + n_f)),
                 pl.BlockSpec((FFN_K, tf), lambda b, i, f: (0, f)),
                 pl.BlockSpec((1, tf), lambda b, i, f: (0, f)),
                 pl.BlockSpec((tf, D_MODEL), lambda b, i, f: (f, 0))]
    args += [hist, norm_g, wup_bf, wup_bf, wdw, bdw, wdown_bf]
    y, tail = pl.pallas_call(
        functools.partial(_ffn_kernel, bs=bs, tl=tl, n_lt=n_lt, tf=tf),
        grid=(n // bs, n_lt, n_f),
        in_specs=in_specs,
        out_specs=[pl.BlockSpec((rows, D_MODEL), lambda b, i, f: (b * n_lt + i, 0)),
                   pl.BlockSpec((bs, F_HALO, tf), lambda b, i, f: (b * n_lt + i, 0, f))],
        out_shape=[jax.ShapeDtypeStruct((n * seq, D_MODEL), F32),
                   jax.ShapeDtypeStruct((n * n_lt, F_HALO, D_FF), F32)],
        scratch_shapes=([pltpu.VMEM((lead + rows, D_MODEL), BF16)]
                        + ([] if n_lt > 1 else [pltpu.VMEM((bs, F_HALO + tl, tf), F32)])
                        + [pltpu.VMEM((rows, D_MODEL), F32)]),
        compiler_params=_params(3),
        name="conv_ffn",
    )(*args)
    tail = tail.reshape(n, n_lt, F_HALO, D_FF)[:, n_lt - 1, F_HALO - (FFN_K - 1):, :]
    return y, tail


def _front_pad(a, rows):
    return jnp.pad(a, ((0, 0), (rows - a.shape[1], 0), (0, 0)))


def _trunk(x, a_cache, b_cache, f_cache, W, sample):
    n, seq, _ = x.shape
    m = n * seq
    tm = 256 if sample else 512
    x2 = x.reshape(m, D_MODEL)
    wid = H_G * DH
    ia = ib = ic = 0
    n_a = (DEPTH + 2) // 3
    new_a = None
    new_b, new_c, new_f = [], [], []
    for layer in range(DEPTH):
        kind = layer % 3
        if kind == 0:
            gains = jnp.stack([W['a_q_gain'][ia], W['a_k_gain'][ia],
                               jnp.ones_like(W['a_q_gain'][ia])], axis=1).reshape(N_GROUPS * 3, 1, DH)
            if not sample:
                gains_qk = jnp.stack([W['a_q_gain'][ia], W['a_k_gain'][ia]], axis=1).reshape(N_GROUPS * 2, 1, DH)
                banded, filled = [], []
                for g in range(N_GROUPS):
                    qkv_g, c_new = _qkv_group_proj(x2, W['a_norm'][ia][None], W['a_w_in'][ia], gains_qk, g, 512, seq,
                                                   ia, None if new_a is None else new_a[g], n_a)
                    banded.append(_attn_band(qkv_g))
                    filled.append(c_new)
                new_a = filled
                x2 = _band_outproj([o for o, _ in banded], [l for _, l in banded], W['a_w_out'][ia], x2, seq, 512)
            else:
                qkv = _qkv_proj(x2, W['a_norm'][ia][None], W['a_w_in'][ia], gains, tm)
                qkv3 = qkv.reshape(n, seq, N_GROUPS * 3 * wid)
                parts, filled = [], []
                for g in range(N_GROUPS):
                    o_g, lse_g, c_new = _attn_sample(qkv3, a_cache[g], ia, g, None if new_a is None else new_a[g])
                    parts += [o_g.reshape(m, wid), lse_g.reshape(m, wid)]
                    filled.append(c_new)
                new_a = filled
                x2 = _merge_outproj(parts, W['a_w_out'][ia], x2)
            ia += 1
        elif kind == 1:
            u = _glu_proj(x2, W['b_norm'][ib][None], W['b_w_pw1'][ib], W['b_b_pw1'][ib][None], tm, 512)
            u3 = u.reshape(n, seq, D_MODEL)
            buf = b_cache[ib] if sample else jnp.zeros((n, CONV_K - 1, D_MODEL), F32)
            hist = _front_pad(buf, B_HALO)
            bs, tl = (n, seq) if sample else (1, 512)
            x2 = _bconv(u3, hist, W['b_w_dw'][ib], W['b_b_dw'][ib][None], W['b_ln_g'][ib][None],
                        W['b_ln_b'][ib][None], W['b_w_pw2'][ib], W['b_b_pw2'][ib][None], x2, bs, tl)
            cat = jnp.concatenate([buf, u3], axis=1)
            new_b.append(cat[:, cat.shape[1] - (CONV_K - 1):])
            ib += 1
        else:
            uv = _gelu_ln_proj(x2, W['c_norm'][ic][None], W['c_w_in'][ic], W['c_ln_g'][ic][None],
                               W['c_ln_b'][ic][None], tm)
            if sample:
                mix = jnp.tile(W['c_w_s'][ic][:, :seq, :seq], (1, n, n))
                brow = jnp.tile(W['c_b_s'][ic].T[:seq], (n, 1))
                rows, l_seq = m, seq
                new_c.append(uv[:, D_MODEL:].reshape(n, seq, D_MODEL))
            else:
                mix = W['c_w_s'][ic]
                brow = W['c_b_s'][ic].T
                rows, l_seq = 512, CHUNK
            x2 = _cgate(uv, mix, brow, W['c_w_out'][ic], x2, rows, l_seq)
            ic += 1
        buf = f_cache[layer] if sample else jnp.zeros((n, FFN_K - 1, D_FF), F32)
        hist = _front_pad(buf, F_HALO)
        bs, tl, tf = (n, seq, 256) if sample else (1, 1024, D_FF // 2)
        x2, tail = _ffn(x2, hist, W['f_norm'][layer][None], W['f_w_up'][layer], W['f_w_dw'][layer],
                        W['f_b_dw'][layer][None], W['f_w_down'][layer], n, seq, bs, tl, tf)
        new_f.append(tail)
    if sample:
        new_a = [a.reshape(n_a, n, a.shape[2] // KV_ROWS, 2, H_G, DH) for a in new_a]
    out = [x2.reshape(n, seq, D_MODEL)] + new_a + [jnp.stack(new_b, axis=0)]
    if sample:
        out.append(jnp.stack(new_c, axis=0))
    out.append(jnp.stack(new_f, axis=0))
    return out


def kernel(x_prompt, x_sample, cache_a_g0_kv, cache_a_g1_kv, cache_a_g2_kv, state_b_conv, state_ffn_conv,
           a_norm, a_w_in, a_q_gain, a_k_gain, a_w_out,
           b_norm, b_w_pw1, b_b_pw1, b_w_dw, b_b_dw, b_ln_g, b_ln_b, b_w_pw2, b_b_pw2,
           c_norm, c_w_in, c_ln_g, c_ln_b, c_w_s, c_b_s, c_w_out,
           f_norm, f_w_up, f_w_dw, f_b_dw, f_w_down):
    W = dict(a_norm=a_norm, a_w_in=a_w_in.astype(BF16), a_q_gain=a_q_gain, a_k_gain=a_k_gain,
             a_w_out=a_w_out.astype(BF16),
             b_norm=b_norm, b_w_pw1=b_w_pw1.astype(BF16), b_b_pw1=b_b_pw1, b_w_dw=b_w_dw, b_b_dw=b_b_dw,
             b_ln_g=b_ln_g, b_ln_b=b_ln_b, b_w_pw2=b_w_pw2.astype(BF16), b_b_pw2=b_b_pw2,
             c_norm=c_norm, c_w_in=c_w_in.astype(BF16), c_ln_g=c_ln_g, c_ln_b=c_ln_b, c_w_s=c_w_s, c_b_s=c_b_s,
             c_w_out=c_w_out.astype(BF16),
             f_norm=f_norm, f_w_up=f_w_up.astype(BF16), f_w_dw=f_w_dw, f_b_dw=f_b_dw,
             f_w_down=f_w_down.astype(BF16))
    yp, pa0, pa1, pa2, pb, pf = _trunk(x_prompt, None, None, None, W, sample=False)
    caches = [c.reshape(c.shape[0], c.shape[1], c.shape[2] * KV_ROWS, DH)
              for c in (cache_a_g0_kv, cache_a_g1_kv, cache_a_g2_kv)]
    ys, sa0, sa1, sa2, sb, sc, sf = _trunk(x_sample, caches, state_b_conv, state_ffn_conv, W, sample=True)
    return (yp, ys, pa0, pa1, pa2, pb, pf, sa0, sa1, sa2, sb, sc, sf)
```
